```python
import jax, jax.numpy as jnp
from jax import lax
import numpy as np

D_MODEL = 2048
BATCH = 4
SEQ = 4096
DEPTH = 4
DEC_BATCH = 16
DEC_SEQ = 32
PAST_LEN = 2048

CHUNK = 64
GMLP_CHUNK = 128
GMLP_HEADS = 8
GMLP_HEAD_DIM = 128
GMLP_WIDTH = GMLP_HEADS * GMLP_HEAD_DIM
HGRN_HEADS = 8
HGRN_DK = 128
HGRN_DV = 128
HGRN_KWIDTH = HGRN_HEADS * HGRN_DK
HGRN_VWIDTH = HGRN_HEADS * HGRN_DV
MIX_WIDTH = GMLP_WIDTH + HGRN_VWIDTH
SPLIT_IDX = (GMLP_WIDTH, 2 * GMLP_WIDTH, 2 * GMLP_WIDTH + HGRN_KWIDTH, 2 * GMLP_WIDTH + 2 * HGRN_KWIDTH,
             2 * GMLP_WIDTH + 2 * HGRN_KWIDTH + HGRN_VWIDTH)
IN_WIDTH = 2 * GMLP_WIDTH + 2 * HGRN_KWIDTH + 2 * HGRN_VWIDTH
D_FF = 5632
N_EXPERTS = 8
TOP_K = 2
D_FF_EXPERT = 7168
N_DENSE = (DEPTH + 1) // 2
N_MOE = DEPTH // 2
EPS = 1e-6
EXP_CLIP = 60.0

kernel_name = 'hymba_style_gmlp_hgrn2_adaln_stream_step'


def rmsnorm(x, gain=None):
    xf = x.astype(jnp.float32)
    y = xf * lax.rsqrt(jnp.mean(xf * xf, axis=-1, keepdims=True) + EPS)
    if gain is not None:
        y = y * gain.astype(jnp.float32)
    return y.astype(x.dtype)


def ada_modulation(c, w_ada, b_ada):
    mod = jax.nn.silu(c) @ w_ada + b_ada
    return [m[:, None, :] for m in jnp.split(mod, 6, axis=-1)]


def hgrn_lower_bounds(lb_logits):
    p = jax.nn.softmax(lb_logits.astype(jnp.float32), axis=0)
    return jnp.cumsum(p, axis=0) - p[0]


def hgrn_chunk(state, q, k, v, log_f):
    c = q.shape[1]
    cum = jnp.cumsum(log_f, axis=1)
    causal = jnp.tril(jnp.ones((c, c), dtype=bool))[None, :, :, None, None]
    diff = jnp.where(causal, cum[:, :, None] - cum[:, None, :], 0.0)
    decay = jnp.where(causal, jnp.exp(diff), 0.0)
    scores = jnp.einsum('bthk,btshk,bshk->bhts', q, decay, k)
    o = (jnp.einsum('bhts,bshv->bthv', scores, v)
         + jnp.einsum('bthk,bhkv->bthv', q * jnp.exp(cum), state))
    last = cum[:, -1]
    k_dec = k * jnp.exp(last[:, None] - cum)
    new_state = jnp.exp(last)[..., None] * state + jnp.einsum('bshk,bshv->bhkv', k_dec, v)
    return new_state, o


def hgrn_recurrence(state, q, k, v, log_f):
    b, l = q.shape[:2]
    c = min(CHUNK, l)
    n = l // c

    def blocks(t):
        return jnp.moveaxis(t.reshape(b, n, c, *t.shape[2:]), 1, 0)

    def step(s, inp):
        return hgrn_chunk(s, *inp)

    s_final, o = lax.scan(step, state, (blocks(q), blocks(k), blocks(v), blocks(log_f)))
    return s_final, jnp.moveaxis(o, 0, 1).reshape(b, l, HGRN_HEADS, HGRN_DV)


def token_mixer(h, hgrn_state, gmlp_pos, w_in, w_out, w_sp, b_sp, v_gain, a_gain, lb, o_gain):
    bsz, l, _ = h.shape
    u, v, q, f_raw, i, g = jnp.split(h @ w_in, SPLIT_IDX, axis=-1)

    lc = gmlp_pos.shape[0]
    v = rmsnorm(v.reshape(bsz, l, GMLP_HEADS, GMLP_HEAD_DIM), v_gain)
    w = w_sp[:, gmlp_pos[:, None], gmlp_pos[None, :]]
    w = jnp.where((gmlp_pos[:, None] >= gmlp_pos[None, :])[None], w, 0.0)
    vb = v.reshape(bsz, l // lc, lc, GMLP_HEADS, GMLP_HEAD_DIM)
    mixed = jnp.einsum('hts,bcshp->bcthp', w, vb) + b_sp[:, gmlp_pos].T[None, None, :, :, None]
    y_a = rmsnorm(u * mixed.reshape(bsz, l, GMLP_WIDTH), a_gain)

    fx = f_raw.astype(jnp.float32).reshape(bsz, l, HGRN_HEADS, HGRN_DK)
    lb = lb.reshape(HGRN_HEADS, HGRN_DK)
    log_f = jax.nn.log_sigmoid(fx) + jnp.log1p(lb * jnp.exp(jnp.minimum(-fx, EXP_CLIP)))
    k = (1.0 - lb) * jax.nn.sigmoid(-fx)
    qh = jax.nn.silu(q.astype(jnp.float32)).reshape(bsz, l, HGRN_HEADS, HGRN_DK)
    ih = i.astype(jnp.float32).reshape(bsz, l, HGRN_HEADS, HGRN_DV)
    new_state, o = hgrn_recurrence(hgrn_state.astype(jnp.float32), qh, k, ih, log_f)
    y_b = rmsnorm(o.astype(h.dtype), o_gain) * jax.nn.silu(g).reshape(bsz, l, HGRN_HEADS, HGRN_DV)

    y = jnp.concatenate([y_a, y_b.reshape(bsz, l, HGRN_VWIDTH)], axis=-1) @ w_out
    return y, new_state, v.reshape(bsz, l, GMLP_WIDTH)


def swiglu(h, w_gate, w_up, w_down):
    return (jax.nn.silu(h @ w_gate) * (h @ w_up)) @ w_down


def moe_swiglu(h, w_router, w_gate, w_up, w_down):
    logits = (h @ w_router).astype(jnp.float32)
    top_vals, top_idx = lax.top_k(logits, TOP_K)
    top_w = jax.nn.softmax(top_vals, axis=-1)
    gates = jnp.sum(jax.nn.one_hot(top_idx, N_EXPERTS, dtype=jnp.float32) * top_w[..., None], axis=-2)
    gates = gates.astype(h.dtype)
    y = jnp.zeros_like(h)
    for e in range(N_EXPERTS):
        y = y + gates[..., e:e + 1] * swiglu(h, w_gate[e], w_up[e], w_down[e])
    return y


def trunk(x, c, hgrn_state, gmlp_pos, keep_rows, lbs, p):
    (w_ada, b_ada, w_in, w_out, w_spatial, b_spatial, gmlp_v_gain, gmlp_out_gain, hgrn_out_gain,
     w_ffn_gate, w_ffn_up, w_ffn_down, w_router, w_moe_gate, w_moe_up, w_moe_down, final_gain) = p
    states, rows = [], []
    for layer in range(DEPTH):
        sh_m, sc_m, g_m, sh_f, sc_f, g_f = ada_modulation(c, w_ada[layer], b_ada[layer])
        h = rmsnorm(x) * (1.0 + sc_m) + sh_m
        y, s_new, v_rows = token_mixer(h, hgrn_state[layer], gmlp_pos, w_in[layer], w_out[layer],
                                       w_spatial[layer], b_spatial[layer], gmlp_v_gain[layer],
                                       gmlp_out_gain[layer], lbs[layer], hgrn_out_gain[layer])
        x = x + g_m * y
        h = rmsnorm(x) * (1.0 + sc_f) + sh_f
        j = layer // 2
        if layer % 2 == 0:
            y = swiglu(h, w_ffn_gate[j], w_ffn_up[j], w_ffn_down[j])
        else:
            y = moe_swiglu(h, w_router[j], w_moe_gate[j], w_moe_up[j], w_moe_down[j])
        x = x + g_f * y
        states.append(s_new.astype(x.dtype))
        if keep_rows:
            rows.append(v_rows)
    y_out = rmsnorm(x, final_gain)
    return y_out, jnp.stack(states), (jnp.stack(rows) if keep_rows else None)


def setup_inputs(seed: int = 0) -> dict:
    key = jax.random.key(seed)
    ks = jax.random.split(key, 24)

    def nrm(k, shape, scale):
        return jax.random.normal(k, shape, jnp.float32) * scale

    return {
        'x_prompt': nrm(ks[0], (BATCH, SEQ, D_MODEL), 1.0),
        'x_sample': nrm(ks[1], (DEC_BATCH, DEC_SEQ, D_MODEL), 1.0),
        'state_hgrn': nrm(ks[2], (DEPTH, DEC_BATCH, HGRN_HEADS, HGRN_DK, HGRN_DV), 0.5),
        'c_prompt': nrm(ks[3], (BATCH, D_MODEL), 1.0),
        'c_sample': nrm(ks[4], (DEC_BATCH, D_MODEL), 1.0),
        'w_ada': nrm(ks[5], (DEPTH, D_MODEL, 6 * D_MODEL), 0.5 * D_MODEL ** -0.5),
        'b_ada': nrm(ks[6], (DEPTH, 6 * D_MODEL), 0.02),
        'w_in': nrm(ks[7], (DEPTH, D_MODEL, IN_WIDTH), D_MODEL ** -0.5),
        'w_out': nrm(ks[8], (DEPTH, MIX_WIDTH, D_MODEL), MIX_WIDTH ** -0.5),
        'w_spatial': nrm(ks[9], (DEPTH, GMLP_HEADS, GMLP_CHUNK, GMLP_CHUNK), 0.5 * GMLP_CHUNK ** -0.5),
        'b_spatial': 1.0 + nrm(ks[10], (DEPTH, GMLP_HEADS, GMLP_CHUNK), 0.1),
        'gmlp_v_gain': 1.0 + nrm(ks[11], (DEPTH, GMLP_HEADS, GMLP_HEAD_DIM), 0.1),
        'gmlp_out_gain': 1.0 + nrm(ks[12], (DEPTH, GMLP_WIDTH), 0.1),
        'hgrn_lb_logits': nrm(ks[13], (DEPTH, HGRN_KWIDTH), 0.5),
        'hgrn_out_gain': 1.0 + nrm(ks[14], (DEPTH, HGRN_HEADS, HGRN_DV), 0.1),
        'w_ffn_gate': nrm(ks[15], (N_DENSE, D_MODEL, D_FF), D_MODEL ** -0.5),
        'w_ffn_up': nrm(ks[16], (N_DENSE, D_MODEL, D_FF), D_MODEL ** -0.5),
        'w_ffn_down': nrm(ks[17], (N_DENSE, D_FF, D_MODEL), D_FF ** -0.5),
        'w_router': nrm(ks[18], (N_MOE, D_MODEL, N_EXPERTS), D_MODEL ** -0.5),
        'w_moe_gate': nrm(ks[19], (N_MOE, N_EXPERTS, D_MODEL, D_FF_EXPERT), D_MODEL ** -0.5),
        'w_moe_up': nrm(ks[20], (N_MOE, N_EXPERTS, D_MODEL, D_FF_EXPERT), D_MODEL ** -0.5),
        'w_moe_down': nrm(ks[21], (N_MOE, N_EXPERTS, D_FF_EXPERT, D_MODEL), D_FF_EXPERT ** -0.5),
        'final_gain': 1.0 + nrm(ks[22], (D_MODEL,), 0.1),
    }


def reference(x_prompt, x_sample, state_hgrn, c_prompt, c_sample, w_ada, b_ada, w_in, w_out,
              w_spatial, b_spatial, gmlp_v_gain, gmlp_out_gain, hgrn_lb_logits, hgrn_out_gain,
              w_ffn_gate, w_ffn_up, w_ffn_down, w_router, w_moe_gate, w_moe_up, w_moe_down, final_gain):
    lbs = hgrn_lower_bounds(hgrn_lb_logits)
    p = (w_ada, b_ada, w_in, w_out, w_spatial, b_spatial, gmlp_v_gain, gmlp_out_gain, hgrn_out_gain,
         w_ffn_gate, w_ffn_up, w_ffn_down, w_router, w_moe_gate, w_moe_up, w_moe_down, final_gain)

    prompt_state0 = jnp.zeros((DEPTH, BATCH, HGRN_HEADS, HGRN_DK, HGRN_DV), x_prompt.dtype)
    prompt_pos = jnp.arange(GMLP_CHUNK)
    y_prompt, new_state_hgrn_prompt, _ = trunk(x_prompt, c_prompt, prompt_state0, prompt_pos, False, lbs, p)

    sample_pos = (PAST_LEN + jnp.arange(DEC_SEQ)) % GMLP_CHUNK
    y_sample, new_state_hgrn_sample, new_gmlp_v_sample = trunk(x_sample, c_sample, state_hgrn, sample_pos,
                                                               True, lbs, p)
    return (y_prompt, y_sample, new_state_hgrn_prompt, new_state_hgrn_sample, new_gmlp_v_sample)
```

```python
import functools

import jax
import jax.numpy as jnp
from jax import lax
from jax.experimental import pallas as pl
from jax.experimental.pallas import tpu as pltpu

F32 = jnp.float32
BF16 = jnp.bfloat16

EPS = 1e-6
EXP_CLIP = 60.0
HGRN_CHUNK = 64
PAST_LEN = 2048
TOP_K = 2

LANES = 128
SUBLANES = 8
VMEM_LIMIT_V7X = 56 * 1024 * 1024

TOKEN_TILE = 512
SLOT_TILE = 512
FF_TILE = 512
IN_TILE = 1024
OUT_TILE = 1024
ADA_TILE = 1024
DMA_ROWS = 512


def _params(*semantics):
    return pltpu.CompilerParams(dimension_semantics=semantics, vmem_limit_bytes=VMEM_LIMIT_V7X)


def _silu(x):
    return x * (1.0 / (1.0 + jnp.exp(-x)))


def _rms(x):
    return x * lax.rsqrt(jnp.mean(x * x, axis=-1, keepdims=True) + EPS)


def _per_group(y, m, group, op):
    r, c = y.shape
    return op(y.reshape(r // group, group, c), m[:, None, :]).reshape(r, c)


def _norm_mod(x, sc, sh, group):
    y = _per_group(_rms(x), 1.0 + sc, group, jnp.multiply)
    return _per_group(y, sh, group, jnp.add)


def _dot(a, b):
    return jnp.dot(a, b, preferred_element_type=F32)


def _dot_nt(a, b):
    return lax.dot_general(a, b, (((1,), (1,)), ((), ())), preferred_element_type=F32)


def _dot_tn(a, b):
    return lax.dot_general(a, b, (((0,), (0,)), ((), ())), preferred_element_type=F32)


def _split3(x):
    hi = x.astype(BF16)
    r = x - hi.astype(F32)
    mid = r.astype(BF16)
    lo = (r - mid.astype(F32)).astype(BF16)
    return hi, mid, lo


def _dot_f32(a, b):
    a1, a2, a3 = _split3(a)
    b1, b2, b3 = _split3(b)
    return (_dot(a1, b1) + (_dot(a1, b2) + _dot(a2, b1))
            + (_dot(a2, b2) + _dot(a1, b3) + _dot(a3, b1)))


def _ada_body(c_ref, w_ref, b_ref, o_ref):
    o_ref[...] = _dot_f32(_silu(c_ref[...]), w_ref[...]) + b_ref[...]


def _ada_modulation(c, w_ada, b_ada):
    depth, d, n = w_ada.shape
    n_streams = c.shape[0]
    s = -(-n_streams // (2 * SUBLANES)) * 2 * SUBLANES
    c = jnp.pad(c, ((0, s - n_streams), (0, 0)))
    tn = min(ADA_TILE, n)
    mod = pl.pallas_call(
        _ada_body,
        grid=(depth, n // tn),
        in_specs=[
            pl.BlockSpec((s, d), lambda l, j: (0, 0)),
            pl.BlockSpec((None, d, tn), lambda l, j: (l, 0, j)),
            pl.BlockSpec((None, 1, tn), lambda l, j: (l, 0, j)),
        ],
        out_specs=pl.BlockSpec((None, s, tn), lambda l, j: (l, 0, j)),
        out_shape=jax.ShapeDtypeStruct((depth, s, n), F32),
        compiler_params=_params("parallel", "parallel"),
        name="ada_modulation",
    )(c, w_ada, b_ada.reshape(depth, 1, n))
    return mod[:, :n_streams]


def _inproj_body(x_ref, sc_ref, sh_ref, w_ref, o_ref, h_scr, *, group):
    @pl.when(pl.program_id(1) == 0)
    def _():
        h_scr[...] = _norm_mod(x_ref[...], sc_ref[...], sh_ref[...], group).astype(BF16)

    o_ref[...] = _dot(h_scr[...], w_ref[...])


def _in_projection(x, modg, layer, w, group):
    t, d = x.shape
    n = w.shape[1]
    tm, tn = TOKEN_TILE, min(IN_TILE, n)
    gm = tm // group
    return pl.pallas_call(
        functools.partial(_inproj_body, group=group),
        grid=(t // tm, n // tn),
        in_specs=[
            pl.BlockSpec((tm, d), lambda m, j: (m, 0)),
            pl.BlockSpec((None, gm, d), lambda m, j: (layer, m, 1)),
            pl.BlockSpec((None, gm, d), lambda m, j: (layer, m, 0)),
            pl.BlockSpec((d, tn), lambda m, j: (0, j)),
        ],
        out_specs=pl.BlockSpec((tm, tn), lambda m, j: (m, j)),
        out_shape=jax.ShapeDtypeStruct((t, n), F32),
        scratch_shapes=[pltpu.VMEM((tm, d), BF16)],
        compiler_params=_params("parallel", "arbitrary"),
        name="in_projection",
    )(x, modg, modg, w)


def _mixer_body(*refs, tb, chunk, n_heads, pos0, has_state0, emit_v):
    (u_ref, v_ref, q_ref, f_ref, i_ref, g_ref, wsp_ref, bsp_ref, vgain_ref, again_ref,
     lb_ref, ogain_ref) = refs[:12]
    refs = refs[12:]
    if has_state0:
        state0_ref, refs = refs[0], refs[1:]
    ymix_ref, state_ref = refs[:2]
    refs = refs[2:]
    if emit_v:
        vout_ref, refs = refs[0], refs[1:]
    s_scr, o_scr, k_scr, c_scr, v_scr = refs

    j = pl.program_id(1)
    width = n_heads * LANES

    @pl.when(j == 0)
    def _():
        if has_state0:
            s_scr[...] = state0_ref[...]
        else:
            s_scr[...] = jnp.zeros_like(s_scr)
        k_scr[pl.ds(0, SUBLANES), :] = jnp.zeros((SUBLANES, width), F32)
        c_scr[pl.ds(0, SUBLANES), :] = jnp.zeros((SUBLANES, width), F32)
        v_scr[pl.ds(0, SUBLANES), :] = jnp.zeros((SUBLANES, width), F32)

    def head(h):
        return slice(h * LANES, (h + 1) * LANES)

    row = lax.broadcasted_iota(jnp.int32, (tb, tb), 0)
    col = lax.broadcasted_iota(jnp.int32, (tb, tb), 1)
    v = v_ref[...]
    u = u_ref[...]
    vgain = vgain_ref[...]
    gated = []
    sq = jnp.zeros((tb, 1), F32)
    for h in range(n_heads):
        vn = _rms(v[:, head(h)]) * vgain[:, head(h)]
        if emit_v:
            vout_ref[:, head(h)] = vn
        w = jnp.where(row >= col, wsp_ref[h, pl.ds(pos0, tb), pl.ds(pos0, tb)], 0.0)
        mixed = _dot(w.astype(BF16), vn.astype(BF16)) + bsp_ref[:, h:h + 1]
        ga = u[:, head(h)] * mixed
        sq = sq + jnp.sum(ga * ga, axis=-1, keepdims=True)
        gated.append(ga)
    inv = lax.rsqrt(sq / width + EPS)
    again = again_ref[...]
    for h in range(n_heads):
        ymix_ref[:, head(h)] = (gated[h] * inv * again[:, head(h)]).astype(BF16)

    lb = lb_ref[...]
    ogain = ogain_ref[...]
    tri = (lax.broadcasted_iota(jnp.int32, (chunk, 3 * chunk), 1) % chunk
           <= lax.broadcasted_iota(jnp.int32, (chunk, 3 * chunk), 0)).astype(BF16)
    rowmod = lax.broadcasted_iota(jnp.int32, (chunk, width), 0) % SUBLANES
    eye = (lax.broadcasted_iota(jnp.int32, (LANES, LANES), 0)
           == lax.broadcasted_iota(jnp.int32, (LANES, LANES), 1))
    for c in range(tb // chunk):
        rows = pl.ds(c * chunk, chunk)
        fx = f_ref[rows, :]
        e = jnp.exp(-jnp.abs(fx))
        log_f = (jnp.minimum(fx, 0.0) - jnp.log1p(e)
                 + jnp.log1p(lb * jnp.exp(jnp.minimum(-fx, EXP_CLIP))))
        kx = (1.0 - lb) * (jnp.where(fx >= 0.0, e, 1.0) / (1.0 + e))
        qs = _silu(q_ref[rows, :])
        vx = i_ref[rows, :]
        cum = _dot(tri, jnp.concatenate(_split3(log_f), axis=0))
        k_scr[pl.ds(SUBLANES, chunk), :] = kx
        c_scr[pl.ds(SUBLANES, chunk), :] = cum
        v_scr[pl.ds(SUBLANES, chunk), :] = vx

        last = cum[chunk - 1:chunk, :]
        qe = (qs * jnp.exp(cum)).astype(BF16)
        kd = (kx * jnp.exp(last - cum)).astype(BF16)
        el = jnp.exp(last)
        vb = vx.astype(BF16)

        acc = [None] * n_heads
        for d in range(SUBLANES):
            if d == 0:
                p, vs = qs * kx, vx
            else:
                ks = k_scr[pl.ds(SUBLANES - d, chunk), :]
                cs = c_scr[pl.ds(SUBLANES - d, chunk), :]
                vs = v_scr[pl.ds(SUBLANES - d, chunk), :]
                p = jnp.where(rowmod >= d, qs * ks * jnp.exp(jnp.minimum(cum - cs, 0.0)), 0.0)
            for h in range(n_heads):
                term = jnp.sum(p[:, head(h)], axis=-1, keepdims=True) * vs[:, head(h)]
                acc[h] = term if acc[h] is None else acc[h] + term

        for h in range(n_heads):
            s_old = s_scr[h]
            o_scr[rows, head(h)] = acc[h] + _dot(qe[:, head(h)], s_old.astype(BF16))
            ecol = jnp.sum(jnp.where(eye, jnp.broadcast_to(el[:, head(h)], (LANES, LANES)), 0.0),
                           axis=1, keepdims=True)
            s_scr[h] = s_old * ecol + _dot_tn(kd[:, head(h)], vb[:, head(h)])

        m = chunk // 2
        while m >= SUBLANES:
            for b in range(chunk // (2 * m)):
                lo, mid = b * 2 * m, b * 2 * m + m
                cref = cum[mid - 1:mid, :]
                a_blk = (qs[mid:mid + m, :] * jnp.exp(cum[mid:mid + m, :] - cref)).astype(BF16)
                b_blk = (kx[lo:mid, :] * jnp.exp(cref - cum[lo:mid, :])).astype(BF16)
                for h in range(n_heads):
                    scores = _dot_nt(a_blk[:, head(h)], b_blk[:, head(h)])
                    o_scr[pl.ds(c * chunk + mid, m), head(h)] += _dot(
                        scores.astype(BF16), vb[lo:mid, head(h)])
            m //= 2

        o = o_scr[rows, :]
        gx = _silu(g_ref[rows, :])
        for h in range(n_heads):
            yb = _rms(o[:, head(h)]) * ogain[:, head(h)] * gx[:, head(h)]
            ymix_ref[rows, pl.ds(width + h * LANES, LANES)] = yb.astype(BF16)

    @pl.when(j == pl.num_programs(1) - 1)
    def _():
        state_ref[...] = s_scr[...]


def _token_mixer(proj, state0, w_sp, b_sp_t, v_gain, a_gain, lb, o_gain, *,
                 n_streams, seq, row0, tb, chunk, n_heads):
    width = n_heads * LANES
    nblk = seq // tb
    blk0 = row0 // tb
    has_state0 = state0 is not None
    emit_v = has_state0
    pos0 = PAST_LEN % w_sp.shape[-1] if has_state0 else 0
    assert pos0 % tb == 0

    def rows_map(col):
        return lambda s, j: (blk0 + s * nblk + j, col)

    const2 = lambda s, j: (0, 0)
    in_specs = [pl.BlockSpec((tb, width), rows_map(k)) for k in range(6)]
    in_specs += [
        pl.BlockSpec(w_sp.shape, lambda s, j: (0, 0, 0)),
        pl.BlockSpec((tb, n_heads), lambda s, j: (pos0 // tb, 0)),
        pl.BlockSpec((1, width), const2),
        pl.BlockSpec((1, width), const2),
        pl.BlockSpec((1, width), const2),
        pl.BlockSpec((1, width), const2),
    ]
    args = [proj] * 6 + [w_sp, b_sp_t, v_gain, a_gain, lb, o_gain]
    if has_state0:
        in_specs.append(pl.BlockSpec((None, n_heads, LANES, LANES), lambda s, j: (s, 0, 0, 0)))
        args.append(state0)
    own_rows = lambda s, j: (s * nblk + j, 0)
    out_shape = [jax.ShapeDtypeStruct((n_streams * seq, 2 * width), BF16),
                 jax.ShapeDtypeStruct((n_streams, n_heads, LANES, LANES), F32)]
    out_specs = [pl.BlockSpec((tb, 2 * width), own_rows),
                 pl.BlockSpec((None, n_heads, LANES, LANES), lambda s, j: (s, 0, 0, 0))]
    if emit_v:
        out_shape.append(jax.ShapeDtypeStruct((n_streams * seq, width), F32))
        out_specs.append(pl.BlockSpec((tb, width), own_rows))

    return pl.pallas_call(
        functools.partial(_mixer_body, tb=tb, chunk=chunk, n_heads=n_heads, pos0=pos0,
                          has_state0=has_state0, emit_v=emit_v),
        grid=(n_streams, nblk),
        in_specs=in_specs,
        out_specs=out_specs,
        out_shape=out_shape,
        scratch_shapes=[
            pltpu.VMEM((n_heads, LANES, LANES), F32),
            pltpu.VMEM((tb, width), F32),
            pltpu.VMEM((SUBLANES + chunk, width), F32),
            pltpu.VMEM((SUBLANES + chunk, width), F32),
            pltpu.VMEM((SUBLANES + chunk, width), F32),
        ],
        compiler_params=_params("parallel", "arbitrary"),
        name="token_mixer_sample" if has_state0 else "token_mixer_prompt",
    )(*args)


def _outproj_body(y_ref, w_ref, x_ref, g_ref, o_ref, *, group):
    y = _dot(y_ref[...], w_ref[...])
    o_ref[...] = x_ref[...] + _per_group(y, g_ref[...], group, jnp.multiply)


def _out_projection(ymix, w, x, modg, layer, group):
    t, k = ymix.shape
    d = w.shape[1]
    tm, tn = TOKEN_TILE, min(OUT_TILE, d)
    ncol = d // tn
    return pl.pallas_call(
        functools.partial(_outproj_body, group=group),
        grid=(t // tm, ncol),
        in_specs=[
            pl.BlockSpec((tm, k), lambda m, j: (m, 0)),
            pl.BlockSpec((k, tn), lambda m, j: (0, j)),
            pl.BlockSpec((tm, tn), lambda m, j: (m, j)),
            pl.BlockSpec((None, tm // group, tn), lambda m, j: (layer, m, 2 * ncol + j)),
        ],
        out_specs=pl.BlockSpec((tm, tn), lambda m, j: (m, j)),
        out_shape=jax.ShapeDtypeStruct((t, d), F32),
        compiler_params=_params("parallel", "arbitrary"),
        name="out_projection",
    )(ymix, w, x, modg)


def _ffn_body(x_ref, sc_ref, sh_ref, wg_ref, wu_ref, wd_ref, gate_ref, o_ref, h_scr, acc_scr, *, group):
    f = pl.program_id(1)

    @pl.when(f == 0)
    def _():
        h_scr[...] = _norm_mod(x_ref[...], sc_ref[...], sh_ref[...], group).astype(BF16)
        acc_scr[...] = jnp.zeros_like(acc_scr)

    h = h_scr[...]
    a = _silu(_dot(h, wg_ref[...])) * _dot(h, wu_ref[...])
    acc_scr[...] += _dot(a.astype(BF16), wd_ref[...])

    @pl.when(f == pl.num_programs(1) - 1)
    def _():
        o_ref[...] = x_ref[...] + _per_group(acc_scr[...], gate_ref[...], group, jnp.multiply)


def _dense_ffn(x, modg, layer, wg, wu, wd, group):
    t, d = x.shape
    ff = wg.shape[1]
    tm, tf = TOKEN_TILE, FF_TILE
    gm = tm // group
    return pl.pallas_call(
        functools.partial(_ffn_body, group=group),
        grid=(t // tm, ff // tf),
        in_specs=[
            pl.BlockSpec((tm, d), lambda m, f: (m, 0)),
            pl.BlockSpec((None, gm, d), lambda m, f: (layer, m, 4)),
            pl.BlockSpec((None, gm, d), lambda m, f: (layer, m, 3)),
            pl.BlockSpec((d, tf), lambda m, f: (0, f)),
            pl.BlockSpec((d, tf), lambda m, f: (0, f)),
            pl.BlockSpec((tf, d), lambda m, f: (f, 0)),
            pl.BlockSpec((None, gm, d), lambda m, f: (layer, m, 5)),
        ],
        out_specs=pl.BlockSpec((tm, d), lambda m, f: (m, 0)),
        out_shape=jax.ShapeDtypeStruct((t, d), F32),
        scratch_shapes=[pltpu.VMEM((tm, d), BF16), pltpu.VMEM((tm, d), F32)],
        compiler_params=_params("parallel", "arbitrary"),
        name="dense_swiglu",
    )(x, modg, modg, wg, wu, wd, modg)


def _route_body(x_ref, sc_ref, sh_ref, wr_ref, h_ref, route_ref, *, group, n_experts):
    tm, d = x_ref.shape
    h = _norm_mod(x_ref[...], sc_ref[...], sh_ref[...], group)
    hf = h.astype(BF16).astype(F32)
    for cb in range(d // LANES):
        h_ref[pl.ds(cb, tm, stride=d // LANES), :] = hf[:, cb * LANES:(cb + 1) * LANES]
    logits = _dot_f32(h, wr_ref[...])
    lane = lax.broadcasted_iota(jnp.int32, logits.shape, 1)
    neg = jnp.float32(-jnp.inf)
    l1 = jnp.where(lane < n_experts, logits, neg)
    m1 = jnp.max(l1, axis=-1, keepdims=True)
    i1 = jnp.min(jnp.where(l1 == m1, lane, LANES), axis=-1, keepdims=True)
    l2 = jnp.where(lane == i1, neg, l1)
    m2 = jnp.max(l2, axis=-1, keepdims=True)
    i2 = jnp.min(jnp.where(l2 == m2, lane, LANES), axis=-1, keepdims=True)
    p = jnp.exp(m2 - m1)
    w1 = 1.0 / (1.0 + p)
    w2 = p / (1.0 + p)
    route_ref[...] = jnp.where(
        lane == 0, i1.astype(F32),
        jnp.where(lane == 1, i2.astype(F32), jnp.where(lane == 2, w1, jnp.where(lane == 3, w2, 0.0))))


def _route(x, modg, layer, w_router_pad, group, n_experts):
    t, d = x.shape
    tm = TOKEN_TILE
    gm = tm // group
    cbs = d // LANES
    return pl.pallas_call(
        functools.partial(_route_body, group=group, n_experts=n_experts),
        grid=(t // tm,),
        in_specs=[
            pl.BlockSpec((tm, d), lambda m: (m, 0)),
            pl.BlockSpec((None, gm, d), lambda m: (layer, m, 4)),
            pl.BlockSpec((None, gm, d), lambda m: (layer, m, 3)),
            pl.BlockSpec((d, LANES), lambda m: (0, 0)),
        ],
        out_specs=[pl.BlockSpec((tm * cbs, LANES), lambda m: (m, 0)),
                   pl.BlockSpec((tm, LANES), lambda m: (m, 0))],
        out_shape=[jax.ShapeDtypeStruct((t * cbs, LANES), F32),
                   jax.ShapeDtypeStruct((t, LANES), F32)],
        compiler_params=_params("parallel"),
        name="route",
    )(x, modg, modg, w_router_pad)


def _token_copy(src, dst, sem, src_tok, dst_tok, cbs, n=1):
    s0 = pl.multiple_of(src_tok * cbs, cbs)
    d0 = pl.multiple_of(dst_tok * cbs, cbs)
    return pltpu.make_async_copy(src.at[pl.ds(s0, n * cbs)], dst.at[pl.ds(d0, n * cbs)], sem)


def _dispatch_body(dest_ref, h_ref, init_ref, hs_ref, sem, *, rows, cbs):
    del init_ref
    base = pl.program_id(0) * rows

    def issue(r, carry):
        for k in range(TOP_K):
            _token_copy(h_ref, hs_ref, sem, r, dest_ref[TOP_K * (base + r) + k], cbs).start()
        return carry

    lax.fori_loop(0, rows, issue, 0)
    for k in range(TOP_K):
        _token_copy(h_ref, hs_ref, sem, 0, 0, cbs, rows).wait()


def _dispatch(h2, dest, n_slots, cbs):
    t = h2.shape[0] // cbs
    rows = DMA_ROWS
    hs0 = jnp.zeros((n_slots * cbs, LANES), h2.dtype)
    return pl.pallas_call(
        functools.partial(_dispatch_body, rows=rows, cbs=cbs),
        grid_spec=pltpu.PrefetchScalarGridSpec(
            num_scalar_prefetch=1,
            grid=(t // rows,),
            in_specs=[pl.BlockSpec((rows * cbs, LANES), lambda m, dest: (m, 0)),
                      pl.BlockSpec(memory_space=pl.ANY)],
            out_specs=pl.BlockSpec(memory_space=pl.ANY),
            scratch_shapes=[pltpu.SemaphoreType.DMA(())],
        ),
        out_shape=jax.ShapeDtypeStruct(hs0.shape, hs0.dtype),
        input_output_aliases={2: 0},
        compiler_params=_params("arbitrary"),
        name="dispatch",
    )(dest, h2, hs0)


def _experts_body(te_ref, nu_ref, hs_ref, wg_ref, wu_ref, wd_ref, o_ref, h_scr, acc_scr):
    del te_ref
    i, f = pl.program_id(0), pl.program_id(1)
    nf = pl.num_programs(1)
    ts, d = h_scr.shape
    cbs = d // LANES
    used = i < nu_ref[0]

    @pl.when(used & (f == 0))
    def _():
        for cb in range(cbs):
            h_scr[:, cb * LANES:(cb + 1) * LANES] = hs_ref[pl.ds(cb, ts, stride=cbs), :].astype(BF16)
        acc_scr[...] = jnp.zeros_like(acc_scr)

    @pl.when(used)
    def _():
        h = h_scr[...]
        a = _silu(_dot(h, wg_ref[...])) * _dot(h, wu_ref[...])
        acc_scr[...] += _dot(a.astype(BF16), wd_ref[...])

    @pl.when(used & (f == nf - 1))
    def _():
        for cb in range(cbs):
            o_ref[pl.ds(cb, ts, stride=cbs), :] = acc_scr[:, cb * LANES:(cb + 1) * LANES]

    @pl.when(jnp.logical_not(used) & (f == nf - 1))
    def _():
        o_ref[...] = jnp.zeros_like(o_ref)


def _experts(hs, tile_expert, n_used, wg, wu, wd):
    n_exp, d, ff = wg.shape
    cbs = d // LANES
    n_slots = hs.shape[0] // cbs
    ts, tf = SLOT_TILE, FF_TILE
    nf = ff // tf

    def fcol(i, f, nu):
        return jnp.where(i < nu[0], f, nf - 1)

    return pl.pallas_call(
        _experts_body,
        grid_spec=pltpu.PrefetchScalarGridSpec(
            num_scalar_prefetch=2,
            grid=(n_slots // ts, nf),
            in_specs=[
                pl.BlockSpec((ts * cbs, LANES), lambda i, f, te, nu: (i, 0)),
                pl.BlockSpec((None, d, tf), lambda i, f, te, nu: (te[i], 0, fcol(i, f, nu))),
                pl.BlockSpec((None, d, tf), lambda i, f, te, nu: (te[i], 0, fcol(i, f, nu))),
                pl.BlockSpec((None, tf, d), lambda i, f, te, nu: (te[i], fcol(i, f, nu), 0)),
            ],
            out_specs=pl.BlockSpec((ts * cbs, LANES), lambda i, f, te, nu: (i, 0)),
            scratch_shapes=[pltpu.VMEM((ts, d), BF16), pltpu.VMEM((ts, d), F32)],
        ),
        out_shape=jax.ShapeDtypeStruct((n_slots * cbs, LANES), F32),
        compiler_params=_params("parallel", "arbitrary"),
        name="experts",
    )(tile_expert, n_used, hs, wg, wu, wd)


def _combine_body(dest_ref, ys_ref, x_ref, route_ref, gate_ref, o_ref, buf, sem, *, group):
    tm, d = x_ref.shape
    cbs = d // LANES
    base = pl.program_id(0) * tm

    def issue(r, carry):
        for k in range(TOP_K):
            _token_copy(ys_ref, buf.at[k], sem, dest_ref[TOP_K * (base + r) + k], r, cbs).start()
        return carry

    lax.fori_loop(0, tm, issue, 0)
    for k in range(TOP_K):
        _token_copy(ys_ref, buf.at[k], sem, 0, 0, cbs, tm).wait()
    w = [route_ref[:, TOP_K + k:TOP_K + k + 1] for k in range(TOP_K)]
    gate = gate_ref[...]
    for cb in range(cbs):
        cols = slice(cb * LANES, (cb + 1) * LANES)
        y = w[0] * buf[0, pl.ds(cb, tm, stride=cbs), :] + w[1] * buf[1, pl.ds(cb, tm, stride=cbs), :]
        o_ref[:, cols] = x_ref[:, cols] + _per_group(y, gate[:, cols], group, jnp.multiply)


def _combine(ys, dest, x, route, modg, layer, group):
    t, d = x.shape
    tm = DMA_ROWS
    cbs = d // LANES
    return pl.pallas_call(
        functools.partial(_combine_body, group=group),
        grid_spec=pltpu.PrefetchScalarGridSpec(
            num_scalar_prefetch=1,
            grid=(t // tm,),
            in_specs=[
                pl.BlockSpec(memory_space=pl.ANY),
                pl.BlockSpec((tm, d), lambda m, dest: (m, 0)),
                pl.BlockSpec((tm, LANES), lambda m, dest: (m, 0)),
                pl.BlockSpec((None, tm // group, d), lambda m, dest: (layer, m, 5)),
            ],
            out_specs=pl.BlockSpec((tm, d), lambda m, dest: (m, 0)),
            scratch_shapes=[pltpu.VMEM((TOP_K, tm * cbs, LANES), F32), pltpu.SemaphoreType.DMA(())],
        ),
        out_shape=jax.ShapeDtypeStruct((t, d), F32),
        compiler_params=_params("arbitrary"),
        name="combine",
    )(dest, ys, x, route, modg)


def _slot_plan(route, n_experts, n_tiles):
    ts = SLOT_TILE
    experts = jnp.arange(n_experts, dtype=jnp.int32)
    onehot = [(route[:, k].astype(jnp.int32)[:, None] == experts[None, :]).astype(jnp.int32)
              for k in range(TOP_K)]
    before = []
    offset = jnp.zeros((n_experts,), jnp.int32)
    for oh in onehot:
        cs = jnp.cumsum(oh, axis=0)
        before.append(offset[None, :] + cs - oh)
        offset = offset + cs[-1]
    padded = (offset + ts - 1) // ts * ts
    ends = jnp.cumsum(padded)
    starts = ends - padded
    dest = jnp.stack([jnp.sum(oh * (starts[None, :] + bf), axis=1) for oh, bf in zip(onehot, before)],
                     axis=1).reshape(-1)
    n_used = (ends[-1] // ts).astype(jnp.int32)
    tile_start = jnp.arange(n_tiles, dtype=jnp.int32) * ts
    tile_start = jnp.minimum(tile_start, (n_used - 1) * ts)
    tile_expert = jnp.sum((tile_start[:, None] >= ends[None, :]).astype(jnp.int32), axis=1)
    return dest, tile_expert, n_used.reshape(1)


def _moe_ffn(x, modg, layer, w_router_pad, wg, wu, wd, group):
    t, d = x.shape
    n_exp = wg.shape[0]
    cbs = d // LANES
    ts = SLOT_TILE
    n_tiles = -(-(TOP_K * t + n_exp * (ts - 1)) // ts)
    h2, route = _route(x, modg, layer, w_router_pad, group, n_exp)
    dest, tile_expert, n_used = _slot_plan(route, n_exp, n_tiles)
    hs = _dispatch(h2, dest, n_tiles * ts, cbs)
    ys = _experts(hs, tile_expert, n_used, wg, wu, wd)
    return _combine(ys, dest, x, route, modg, layer, group)


def _final_body(x_ref, gain_ref, o_ref):
    o_ref[...] = _rms(x_ref[...]) * gain_ref[...]


def _final_norm(x, gain):
    t, d = x.shape
    tm = TOKEN_TILE
    return pl.pallas_call(
        _final_body,
        grid=(t // tm,),
        in_specs=[pl.BlockSpec((tm, d), lambda m: (m, 0)), pl.BlockSpec((1, d), lambda m: (0, 0))],
        out_specs=pl.BlockSpec((tm, d), lambda m: (m, 0)),
        out_shape=jax.ShapeDtypeStruct((t, d), F32),
        compiler_params=_params("parallel"),
        name="final_norm",
    )(x, gain.reshape(1, d))


def kernel(x_prompt, x_sample, state_hgrn, c_prompt, c_sample, w_ada, b_ada, w_in, w_out, w_spatial, b_spatial, gmlp_v_gain, gmlp_out_gain, hgrn_lb_logits, hgrn_out_gain, w_ffn_gate, w_ffn_up, w_ffn_down, w_router, w_moe_gate, w_moe_up, w_moe_down, final_gain):
    batch, seq, d = x_prompt.shape
    dec_batch, dec_seq, _ = x_sample.shape
    depth = w_in.shape[0]
    n_heads = w_spatial.shape[1]
    gmlp_chunk = w_spatial.shape[2]
    width = n_heads * LANES
    n_exp = w_router.shape[-1]
    group = dec_seq
    t_prompt = batch * seq
    t = t_prompt + dec_batch * dec_seq
    assert state_hgrn.shape[2:] == (n_heads, LANES, LANES) and gmlp_v_gain.shape[1:] == (n_heads, LANES)
    assert w_in.shape[2] == 6 * width and seq % gmlp_chunk == 0 and gmlp_chunk % HGRN_CHUNK == 0
    assert seq % group == 0 and TOKEN_TILE % group == 0 and t % TOKEN_TILE == 0 and t % DMA_ROWS == 0
    assert dec_seq <= HGRN_CHUNK and PAST_LEN % gmlp_chunk + dec_seq <= gmlp_chunk

    x = jnp.concatenate([x_prompt.reshape(t_prompt, d), x_sample.reshape(-1, d)], axis=0)
    c = jnp.concatenate([c_prompt, c_sample], axis=0)
    mod = _ada_modulation(c, w_ada, b_ada)
    mod_prompt = jnp.broadcast_to(mod[:, :batch, None, :], (depth, batch, seq // group, mod.shape[-1]))
    modg = jnp.concatenate([mod_prompt.reshape(depth, t_prompt // group, -1), mod[:, batch:]], axis=1)

    p = jax.nn.softmax(hgrn_lb_logits.astype(F32), axis=0)
    lbs = jnp.cumsum(p, axis=0) - p[0]
    w_router_pad = jnp.pad(w_router, ((0, 0), (0, 0), (0, LANES - n_exp)))
    b_sp_t = jnp.swapaxes(b_spatial, 1, 2)

    states_p, states_s, v_rows = [], [], []
    for layer in range(depth):
        proj = _in_projection(x, modg, layer, w_in[layer].astype(BF16), group)
        small = (w_spatial[layer], b_sp_t[layer], gmlp_v_gain[layer].reshape(1, width),
                 gmlp_out_gain[layer].reshape(1, width), lbs[layer].reshape(1, width),
                 hgrn_out_gain[layer].reshape(1, width))
        y_p, s_p = _token_mixer(proj, None, *small, n_streams=batch, seq=seq, row0=0,
                                tb=gmlp_chunk, chunk=HGRN_CHUNK, n_heads=n_heads)
        y_s, s_s, v_s = _token_mixer(proj, state_hgrn[layer], *small, n_streams=dec_batch,
                                     seq=dec_seq, row0=t_prompt, tb=dec_seq, chunk=dec_seq,
                                     n_heads=n_heads)
        ymix = jnp.concatenate([y_p, y_s], axis=0)
        x = _out_projection(ymix, w_out[layer].astype(BF16), x, modg, layer, group)
        j = layer // 2
        if layer % 2 == 0:
            x = _dense_ffn(x, modg, layer, w_ffn_gate[j].astype(BF16), w_ffn_up[j].astype(BF16),
                           w_ffn_down[j].astype(BF16), group)
        else:
            x = _moe_ffn(x, modg, layer, w_router_pad[j], w_moe_gate[j].astype(BF16),
                         w_moe_up[j].astype(BF16), w_moe_down[j].astype(BF16), group)
        states_p.append(s_p)
        states_s.append(s_s)
        v_rows.append(v_s.reshape(dec_batch, dec_seq, width))

    y = _final_norm(x, final_gain)
    return (y[:t_prompt].reshape(batch, seq, d), y[t_prompt:].reshape(dec_batch, dec_seq, d),
            jnp.stack(states_p), jnp.stack(states_s), jnp.stack(v_rows))
```

```python
import functools

import jax
import jax.numpy as jnp
from jax import lax
from jax.experimental import pallas as pl
from jax.experimental.pallas import tpu as pltpu

F32 = jnp.float32
BF16 = jnp.bfloat16

EPS = 1e-6
EXP_CLIP = 60.0
HGRN_CHUNK = 64
PAST_LEN = 2048
TOP_K = 2

LANES = 128
SUBLANES = 8
VMEM_LIMIT_V7X = 56 * 1024 * 1024

TOKEN_TILE = 768
SLOT_TILE = 768
NORM_TILE = 512
PROLOGUE_ROWS = 256
CAST_BLOCK_BYTES = 8 * 1024 * 1024
FF_TILE = 512
IN_TILE = 1024
OUT_TILE = 1024
ADA_TILE = 1024
DMA_ROWS = 512


def _params(*semantics):
    return pltpu.CompilerParams(dimension_semantics=semantics, vmem_limit_bytes=VMEM_LIMIT_V7X)


def _silu(x):
    return x * (1.0 / (1.0 + jnp.exp(-x)))


def _rms(x):
    return x * lax.rsqrt(jnp.mean(x * x, axis=-1, keepdims=True) + EPS)


def _per_group(y, m, group, op):
    r, c = y.shape
    return op(y.reshape(r // group, group, c), m[:, None, :]).reshape(r, c)


def _norm_mod(x, sc, sh, group):
    y = _per_group(_rms(x), 1.0 + sc, group, jnp.multiply)
    return _per_group(y, sh, group, jnp.add)


def _row_chunks(n_rows, group):
    size = PROLOGUE_ROWS if n_rows % PROLOGUE_ROWS == 0 else n_rows
    assert size % (group * SUBLANES) == 0 or size == n_rows
    return [(r, size) for r in range(0, n_rows, size)]


def _norm_mod_rows(x_ref, sc_ref, sh_ref, group, r0, nr):
    g0, ng = r0 // group, nr // group
    return _norm_mod(x_ref[pl.ds(r0, nr), :], sc_ref[pl.ds(g0, ng), :], sh_ref[pl.ds(g0, ng), :], group)


def _dot(a, b):
    return jnp.dot(a, b, preferred_element_type=F32)


def _dot_nt(a, b):
    return lax.dot_general(a, b, (((1,), (1,)), ((), ())), preferred_element_type=F32)


def _dot_tn(a, b):
    return lax.dot_general(a, b, (((0,), (0,)), ((), ())), preferred_element_type=F32)


def _split3(x):
    hi = x.astype(BF16)
    r = x - hi.astype(F32)
    mid = r.astype(BF16)
    lo = (r - mid.astype(F32)).astype(BF16)
    return hi, mid, lo


def _dot_f32(a, b):
    a1, a2, a3 = _split3(a)
    b1, b2, b3 = b if isinstance(b, tuple) else _split3(b)
    return (_dot(a1, b1) + (_dot(a1, b2) + _dot(a2, b1))
            + (_dot(a2, b2) + _dot(a1, b3) + _dot(a3, b1)))


def _ada_body(c_ref, w_ref, b_ref, o_ref):
    o_ref[...] = _dot_f32(_silu(c_ref[...]), w_ref[...]) + b_ref[...]


def _ada_modulation(c, w_ada, b_ada):
    depth, d, n = w_ada.shape
    n_streams = c.shape[0]
    s = -(-n_streams // (2 * SUBLANES)) * 2 * SUBLANES
    c = jnp.pad(c, ((0, s - n_streams), (0, 0)))
    tn = min(ADA_TILE, n)
    mod = pl.pallas_call(
        _ada_body,
        grid=(depth, n // tn),
        in_specs=[
            pl.BlockSpec((s, d), lambda l, j: (0, 0)),
            pl.BlockSpec((None, d, tn), lambda l, j: (l, 0, j)),
            pl.BlockSpec((None, 1, tn), lambda l, j: (l, 0, j)),
        ],
        out_specs=pl.BlockSpec((None, s, tn), lambda l, j: (l, 0, j)),
        out_shape=jax.ShapeDtypeStruct((depth, s, n), F32),
        compiler_params=_params("parallel", "parallel"),
        name="ada_modulation",
    )(c, w_ada, b_ada.reshape(depth, 1, n))
    return mod[:, :n_streams]


def _inproj_body(x_ref, sc_ref, sh_ref, w_ref, o_ref, h_scr, *, group):
    @pl.when(pl.program_id(1) == 0)
    def _():
        for r0, nr in _row_chunks(x_ref.shape[0], group):
            h_scr[pl.ds(r0, nr), :] = _norm_mod_rows(x_ref, sc_ref, sh_ref, group, r0, nr).astype(BF16)

    o_ref[...] = _dot(h_scr[...], w_ref[...])


def _in_projection(x, modg, layer, w, group):
    t, d = x.shape
    n = w.shape[1]
    tm, tn = TOKEN_TILE, min(IN_TILE, n)
    gm = tm // group
    return pl.pallas_call(
        functools.partial(_inproj_body, group=group),
        grid=(t // tm, n // tn),
        in_specs=[
            pl.BlockSpec((tm, d), lambda m, j: (m, 0)),
            pl.BlockSpec((None, gm, d), lambda m, j: (layer, m, 1)),
            pl.BlockSpec((None, gm, d), lambda m, j: (layer, m, 0)),
            pl.BlockSpec((d, tn), lambda m, j: (0, j)),
        ],
        out_specs=pl.BlockSpec((tm, tn), lambda m, j: (m, j)),
        out_shape=jax.ShapeDtypeStruct((t, n), F32),
        scratch_shapes=[pltpu.VMEM((tm, d), BF16)],
        compiler_params=_params("parallel", "arbitrary"),
        name="in_projection",
    )(x, modg, modg, w)


def _mixer_body(*refs, tb, chunk, n_heads, pos0, has_state0, emit_v):
    (u_ref, v_ref, q_ref, f_ref, i_ref, g_ref, wsp_ref, bsp_ref, vgain_ref, again_ref,
     lb_ref, ogain_ref) = refs[:12]
    refs = refs[12:]
    if has_state0:
        state0_ref, refs = refs[0], refs[1:]
    ymix_ref, state_ref = refs[:2]
    refs = refs[2:]
    if emit_v:
        vout_ref, refs = refs[0], refs[1:]
    s_scr, o_scr, k_scr, c_scr, v_scr = refs

    j = pl.program_id(1)
    width = n_heads * LANES

    @pl.when(j == 0)
    def _():
        if has_state0:
            s_scr[...] = state0_ref[...]
        else:
            s_scr[...] = jnp.zeros_like(s_scr)
        k_scr[pl.ds(0, SUBLANES), :] = jnp.zeros((SUBLANES, width), F32)
        c_scr[pl.ds(0, SUBLANES), :] = jnp.zeros((SUBLANES, width), F32)
        v_scr[pl.ds(0, SUBLANES), :] = jnp.zeros((SUBLANES, width), F32)

    def head(h):
        return slice(h * LANES, (h + 1) * LANES)

    row = lax.broadcasted_iota(jnp.int32, (tb, tb), 0)
    col = lax.broadcasted_iota(jnp.int32, (tb, tb), 1)
    v = v_ref[...]
    u = u_ref[...]
    vgain = vgain_ref[...]
    gated = []
    sq = jnp.zeros((tb, 1), F32)
    for h in range(n_heads):
        vn = _rms(v[:, head(h)]) * vgain[:, head(h)]
        if emit_v:
            vout_ref[:, head(h)] = vn
        w = jnp.where(row >= col, wsp_ref[h, pl.ds(pos0, tb), pl.ds(pos0, tb)], 0.0)
        mixed = _dot(w.astype(BF16), vn.astype(BF16)) + bsp_ref[:, h:h + 1]
        ga = u[:, head(h)] * mixed
        sq = sq + jnp.sum(ga * ga, axis=-1, keepdims=True)
        gated.append(ga)
    inv = lax.rsqrt(sq / width + EPS)
    again = again_ref[...]
    for h in range(n_heads):
        ymix_ref[:, head(h)] = (gated[h] * inv * again[:, head(h)]).astype(BF16)

    lb = lb_ref[...]
    ogain = ogain_ref[...]
    tri = (lax.broadcasted_iota(jnp.int32, (chunk, 3 * chunk), 1) % chunk
           <= lax.broadcasted_iota(jnp.int32, (chunk, 3 * chunk), 0)).astype(BF16)
    rowmod = lax.broadcasted_iota(jnp.int32, (chunk, width), 0) % SUBLANES
    eye = (lax.broadcasted_iota(jnp.int32, (LANES, LANES), 0)
           == lax.broadcasted_iota(jnp.int32, (LANES, LANES), 1))
    for c in range(tb // chunk):
        rows = pl.ds(c * chunk, chunk)
        fx = f_ref[rows, :]
        e = jnp.exp(-jnp.abs(fx))
        log_f = (jnp.minimum(fx, 0.0) - jnp.log1p(e)
                 + jnp.log1p(lb * jnp.exp(jnp.minimum(-fx, EXP_CLIP))))
        kx = (1.0 - lb) * (jnp.where(fx >= 0.0, e, 1.0) / (1.0 + e))
        qs = _silu(q_ref[rows, :])
        vx = i_ref[rows, :]
        cum = _dot(tri, jnp.concatenate(_split3(log_f), axis=0))
        k_scr[pl.ds(SUBLANES, chunk), :] = kx
        c_scr[pl.ds(SUBLANES, chunk), :] = cum
        v_scr[pl.ds(SUBLANES, chunk), :] = vx

        last = cum[chunk - 1:chunk, :]
        qe = (qs * jnp.exp(cum)).astype(BF16)
        kd = (kx * jnp.exp(last - cum)).astype(BF16)
        el = jnp.exp(last)
        vb = vx.astype(BF16)

        acc = [None] * n_heads
        for d in range(SUBLANES):
            if d == 0:
                p, vs = qs * kx, vx
            else:
                ks = k_scr[pl.ds(SUBLANES - d, chunk), :]
                cs = c_scr[pl.ds(SUBLANES - d, chunk), :]
                vs = v_scr[pl.ds(SUBLANES - d, chunk), :]
                p = jnp.where(rowmod >= d, qs * ks * jnp.exp(jnp.minimum(cum - cs, 0.0)), 0.0)
            for h in range(n_heads):
                term = jnp.sum(p[:, head(h)], axis=-1, keepdims=True) * vs[:, head(h)]
                acc[h] = term if acc[h] is None else acc[h] + term

        for h in range(n_heads):
            s_old = s_scr[h]
            o_scr[rows, head(h)] = acc[h] + _dot(qe[:, head(h)], s_old.astype(BF16))
            ecol = jnp.sum(jnp.where(eye, jnp.broadcast_to(el[:, head(h)], (LANES, LANES)), 0.0),
                           axis=1, keepdims=True)
            s_scr[h] = s_old * ecol + _dot_tn(kd[:, head(h)], vb[:, head(h)])

        m = chunk // 2
        while m >= SUBLANES:
            for b in range(chunk // (2 * m)):
                lo, mid = b * 2 * m, b * 2 * m + m
                cref = cum[mid - 1:mid, :]
                a_blk = (qs[mid:mid + m, :] * jnp.exp(cum[mid:mid + m, :] - cref)).astype(BF16)
                b_blk = (kx[lo:mid, :] * jnp.exp(cref - cum[lo:mid, :])).astype(BF16)
                for h in range(n_heads):
                    scores = _dot_nt(a_blk[:, head(h)], b_blk[:, head(h)])
                    o_scr[pl.ds(c * chunk + mid, m), head(h)] += _dot(
                        scores.astype(BF16), vb[lo:mid, head(h)])
            m //= 2

        o = o_scr[rows, :]
        gx = _silu(g_ref[rows, :])
        for h in range(n_heads):
            yb = _rms(o[:, head(h)]) * ogain[:, head(h)] * gx[:, head(h)]
            ymix_ref[rows, pl.ds(width + h * LANES, LANES)] = yb.astype(BF16)

    @pl.when(j == pl.num_programs(1) - 1)
    def _():
        state_ref[...] = s_scr[...]


def _token_mixer(proj, state0, w_sp, b_sp_t, v_gain, a_gain, lb, o_gain, *,
                 n_streams, seq, row0, tb, chunk, n_heads):
    width = n_heads * LANES
    nblk = seq // tb
    blk0 = row0 // tb
    has_state0 = state0 is not None
    emit_v = has_state0
    pos0 = PAST_LEN % w_sp.shape[-1] if has_state0 else 0
    assert pos0 % tb == 0

    def rows_map(col):
        return lambda s, j: (blk0 + s * nblk + j, col)

    const2 = lambda s, j: (0, 0)
    in_specs = [pl.BlockSpec((tb, width), rows_map(k)) for k in range(6)]
    in_specs += [
        pl.BlockSpec(w_sp.shape, lambda s, j: (0, 0, 0)),
        pl.BlockSpec((tb, n_heads), lambda s, j: (pos0 // tb, 0)),
        pl.BlockSpec((1, width), const2),
        pl.BlockSpec((1, width), const2),
        pl.BlockSpec((1, width), const2),
        pl.BlockSpec((1, width), const2),
    ]
    args = [proj] * 6 + [w_sp, b_sp_t, v_gain, a_gain, lb, o_gain]
    if has_state0:
        in_specs.append(pl.BlockSpec((None, n_heads, LANES, LANES), lambda s, j: (s, 0, 0, 0)))
        args.append(state0)
    own_rows = lambda s, j: (s * nblk + j, 0)
    out_shape = [jax.ShapeDtypeStruct((n_streams * seq, 2 * width), BF16),
                 jax.ShapeDtypeStruct((n_streams, n_heads, LANES, LANES), F32)]
    out_specs = [pl.BlockSpec((tb, 2 * width), own_rows),
                 pl.BlockSpec((None, n_heads, LANES, LANES), lambda s, j: (s, 0, 0, 0))]
    if emit_v:
        out_shape.append(jax.ShapeDtypeStruct((n_streams * seq, width), F32))
        out_specs.append(pl.BlockSpec((tb, width), own_rows))

    return pl.pallas_call(
        functools.partial(_mixer_body, tb=tb, chunk=chunk, n_heads=n_heads, pos0=pos0,
                          has_state0=has_state0, emit_v=emit_v),
        grid=(n_streams, nblk),
        in_specs=in_specs,
        out_specs=out_specs,
        out_shape=out_shape,
        scratch_shapes=[
            pltpu.VMEM((n_heads, LANES, LANES), F32),
            pltpu.VMEM((tb, width), F32),
            pltpu.VMEM((SUBLANES + chunk, width), F32),
            pltpu.VMEM((SUBLANES + chunk, width), F32),
            pltpu.VMEM((SUBLANES + chunk, width), F32),
        ],
        compiler_params=_params("parallel", "arbitrary"),
        name="token_mixer_sample" if has_state0 else "token_mixer_prompt",
    )(*args)


def _outproj_body(y_ref, w_ref, x_ref, g_ref, o_ref, *, group):
    y = _dot(y_ref[...], w_ref[...])
    o_ref[...] = x_ref[...] + _per_group(y, g_ref[...], group, jnp.multiply)


def _out_projection(ymix, w, x, modg, layer, group):
    t, k = ymix.shape
    d = w.shape[1]
    tm, tn = TOKEN_TILE, min(OUT_TILE, d)
    ncol = d // tn
    return pl.pallas_call(
        functools.partial(_outproj_body, group=group),
        grid=(t // tm, ncol),
        in_specs=[
            pl.BlockSpec((tm, k), lambda m, j: (m, 0)),
            pl.BlockSpec((k, tn), lambda m, j: (0, j)),
            pl.BlockSpec((tm, tn), lambda m, j: (m, j)),
            pl.BlockSpec((None, tm // group, tn), lambda m, j: (layer, m, 2 * ncol + j)),
        ],
        out_specs=pl.BlockSpec((tm, tn), lambda m, j: (m, j)),
        out_shape=jax.ShapeDtypeStruct((t, d), F32),
        compiler_params=_params("parallel", "arbitrary"),
        name="out_projection",
    )(ymix, w, x, modg)


def _ffn_body(x_ref, sc_ref, sh_ref, wg_ref, wu_ref, wd_ref, gate_ref, o_ref, h_scr, *, group):
    f = pl.program_id(1)

    @pl.when(f == 0)
    def _():
        for r0, nr in _row_chunks(x_ref.shape[0], group):
            h_scr[pl.ds(r0, nr), :] = _norm_mod_rows(x_ref, sc_ref, sh_ref, group, r0, nr).astype(BF16)
        o_ref[...] = jnp.zeros_like(o_ref)

    h = h_scr[...]
    a = _silu(_dot(h, wg_ref[...])) * _dot(h, wu_ref[...])
    o_ref[...] += _dot(a.astype(BF16), wd_ref[...])

    @pl.when(f == pl.num_programs(1) - 1)
    def _():
        for r0, nr in _row_chunks(x_ref.shape[0], group):
            rows, grows = pl.ds(r0, nr), pl.ds(r0 // group, nr // group)
            o_ref[rows, :] = x_ref[rows, :] + _per_group(o_ref[rows, :], gate_ref[grows, :], group,
                                                         jnp.multiply)


def _dense_ffn(x, modg, layer, wg, wu, wd, group):
    t, d = x.shape
    ff = wg.shape[1]
    tm, tf = TOKEN_TILE, FF_TILE
    gm = tm // group
    return pl.pallas_call(
        functools.partial(_ffn_body, group=group),
        grid=(t // tm, ff // tf),
        in_specs=[
            pl.BlockSpec((tm, d), lambda m, f: (m, 0)),
            pl.BlockSpec((None, gm, d), lambda m, f: (layer, m, 4)),
            pl.BlockSpec((None, gm, d), lambda m, f: (layer, m, 3)),
            pl.BlockSpec((d, tf), lambda m, f: (0, f)),
            pl.BlockSpec((d, tf), lambda m, f: (0, f)),
            pl.BlockSpec((tf, d), lambda m, f: (f, 0)),
            pl.BlockSpec((None, gm, d), lambda m, f: (layer, m, 5)),
        ],
        out_specs=pl.BlockSpec((tm, d), lambda m, f: (m, 0)),
        out_shape=jax.ShapeDtypeStruct((t, d), F32),
        scratch_shapes=[pltpu.VMEM((tm, d), BF16)],
        compiler_params=_params("parallel", "arbitrary"),
        name="dense_swiglu",
    )(x, modg, modg, wg, wu, wd, modg)


def _route_body(x_ref, sc_ref, sh_ref, wr_ref, h_ref, route_ref, *, group, n_experts):
    tm, d = x_ref.shape
    cbs = d // LANES
    wr = _split3(wr_ref[...])
    for r0, nr in _row_chunks(tm, group):
        h = _norm_mod_rows(x_ref, sc_ref, sh_ref, group, r0, nr)
        hf = h.astype(BF16).astype(F32)
        for cb in range(cbs):
            h_ref[pl.ds(r0 * cbs + cb, nr, stride=cbs), :] = hf[:, cb * LANES:(cb + 1) * LANES]
        logits = _dot_f32(h, wr)
        lane = lax.broadcasted_iota(jnp.int32, logits.shape, 1)
        neg = jnp.float32(-jnp.inf)
        l1 = jnp.where(lane < n_experts, logits, neg)
        m1 = jnp.max(l1, axis=-1, keepdims=True)
        i1 = jnp.min(jnp.where(l1 == m1, lane, LANES), axis=-1, keepdims=True)
        l2 = jnp.where(lane == i1, neg, l1)
        m2 = jnp.max(l2, axis=-1, keepdims=True)
        i2 = jnp.min(jnp.where(l2 == m2, lane, LANES), axis=-1, keepdims=True)
        p = jnp.exp(m2 - m1)
        w1 = 1.0 / (1.0 + p)
        w2 = p / (1.0 + p)
        route_ref[pl.ds(r0, nr), :] = jnp.where(
            lane == 0, i1.astype(F32),
            jnp.where(lane == 1, i2.astype(F32), jnp.where(lane == 2, w1, jnp.where(lane == 3, w2, 0.0))))


def _route(x, modg, layer, w_router_pad, group, n_experts):
    t, d = x.shape
    tm = TOKEN_TILE
    gm = tm // group
    cbs = d // LANES
    return pl.pallas_call(
        functools.partial(_route_body, group=group, n_experts=n_experts),
        grid=(t // tm,),
        in_specs=[
            pl.BlockSpec((tm, d), lambda m: (m, 0)),
            pl.BlockSpec((None, gm, d), lambda m: (layer, m, 4)),
            pl.BlockSpec((None, gm, d), lambda m: (layer, m, 3)),
            pl.BlockSpec((d, LANES), lambda m: (0, 0)),
        ],
        out_specs=[pl.BlockSpec((tm * cbs, LANES), lambda m: (m, 0)),
                   pl.BlockSpec((tm, LANES), lambda m: (m, 0))],
        out_shape=[jax.ShapeDtypeStruct((t * cbs, LANES), F32),
                   jax.ShapeDtypeStruct((t, LANES), F32)],
        compiler_params=_params("parallel"),
        name="route",
    )(x, modg, modg, w_router_pad)


def _token_copy(src, dst, sem, src_tok, dst_tok, cbs, n=1):
    s0 = pl.multiple_of(src_tok * cbs, cbs)
    d0 = pl.multiple_of(dst_tok * cbs, cbs)
    return pltpu.make_async_copy(src.at[pl.ds(s0, n * cbs)], dst.at[pl.ds(d0, n * cbs)], sem)


def _dispatch_body(dest_ref, h_ref, init_ref, hs_ref, sem, *, rows, cbs):
    del init_ref
    base = pl.program_id(0) * rows

    def issue(r, carry):
        for k in range(TOP_K):
            _token_copy(h_ref, hs_ref, sem, r, dest_ref[TOP_K * (base + r) + k], cbs).start()
        return carry

    lax.fori_loop(0, rows, issue, 0)
    for k in range(TOP_K):
        _token_copy(h_ref, hs_ref, sem, 0, 0, cbs, rows).wait()


def _dispatch(h2, dest, n_slots, cbs):
    t = h2.shape[0] // cbs
    rows = DMA_ROWS
    hs0 = jnp.zeros((n_slots * cbs, LANES), h2.dtype)
    return pl.pallas_call(
        functools.partial(_dispatch_body, rows=rows, cbs=cbs),
        grid_spec=pltpu.PrefetchScalarGridSpec(
            num_scalar_prefetch=1,
            grid=(t // rows,),
            in_specs=[pl.BlockSpec((rows * cbs, LANES), lambda m, dest: (m, 0)),
                      pl.BlockSpec(memory_space=pl.ANY)],
            out_specs=pl.BlockSpec(memory_space=pl.ANY),
            scratch_shapes=[pltpu.SemaphoreType.DMA(())],
        ),
        out_shape=jax.ShapeDtypeStruct(hs0.shape, hs0.dtype),
        input_output_aliases={2: 0},
        compiler_params=_params("arbitrary"),
        name="dispatch",
    )(dest, h2, hs0)


def _experts_body(te_ref, nu_ref, hs_ref, wg_ref, wu_ref, wd_ref, o_ref, h_scr, acc_scr):
    del te_ref
    i, f = pl.program_id(0), pl.program_id(1)
    nf = pl.num_programs(1)
    ts, d = h_scr.shape
    cbs = d // LANES
    used = i < nu_ref[0]

    @pl.when(used & (f == 0))
    def _():
        for cb in range(cbs):
            h_scr[:, cb * LANES:(cb + 1) * LANES] = hs_ref[pl.ds(cb, ts, stride=cbs), :].astype(BF16)
        acc_scr[...] = jnp.zeros_like(acc_scr)

    @pl.when(used)
    def _():
        h = h_scr[...]
        a = _silu(_dot(h, wg_ref[...])) * _dot(h, wu_ref[...])
        acc_scr[...] += _dot(a.astype(BF16), wd_ref[...])

    @pl.when(used & (f == nf - 1))
    def _():
        for cb in range(cbs):
            o_ref[pl.ds(cb, ts, stride=cbs), :] = acc_scr[:, cb * LANES:(cb + 1) * LANES]

    @pl.when(jnp.logical_not(used) & (f == nf - 1))
    def _():
        o_ref[...] = jnp.zeros_like(o_ref)


def _experts(hs, tile_expert, n_used, wg, wu, wd):
    n_exp, d, ff = wg.shape
    cbs = d // LANES
    n_slots = hs.shape[0] // cbs
    ts, tf = SLOT_TILE, FF_TILE
    nf = ff // tf

    def fcol(i, f, nu):
        return jnp.where(i < nu[0], f, nf - 1)

    return pl.pallas_call(
        _experts_body,
        grid_spec=pltpu.PrefetchScalarGridSpec(
            num_scalar_prefetch=2,
            grid=(n_slots // ts, nf),
            in_specs=[
                pl.BlockSpec((ts * cbs, LANES), lambda i, f, te, nu: (i, 0)),
                pl.BlockSpec((None, d, tf), lambda i, f, te, nu: (te[i], 0, fcol(i, f, nu))),
                pl.BlockSpec((None, d, tf), lambda i, f, te, nu: (te[i], 0, fcol(i, f, nu))),
                pl.BlockSpec((None, tf, d), lambda i, f, te, nu: (te[i], fcol(i, f, nu), 0)),
            ],
            out_specs=pl.BlockSpec((ts * cbs, LANES), lambda i, f, te, nu: (i, 0)),
            scratch_shapes=[pltpu.VMEM((ts, d), BF16), pltpu.VMEM((ts, d), F32)],
        ),
        out_shape=jax.ShapeDtypeStruct((n_slots * cbs, LANES), F32),
        compiler_params=_params("parallel", "arbitrary"),
        name="experts",
    )(tile_expert, n_used, hs, wg, wu, wd)


def _combine_body(dest_ref, ys_ref, x_ref, route_ref, gate_ref, o_ref, buf, sem, *, group):
    tm, d = x_ref.shape
    cbs = d // LANES
    base = pl.program_id(0) * tm

    def issue(r, carry):
        for k in range(TOP_K):
            _token_copy(ys_ref, buf.at[k], sem, dest_ref[TOP_K * (base + r) + k], r, cbs).start()
        return carry

    lax.fori_loop(0, tm, issue, 0)
    for k in range(TOP_K):
        _token_copy(ys_ref, buf.at[k], sem, 0, 0, cbs, tm).wait()
    w = [route_ref[:, TOP_K + k:TOP_K + k + 1] for k in range(TOP_K)]
    gate = gate_ref[...]
    for cb in range(cbs):
        cols = slice(cb * LANES, (cb + 1) * LANES)
        y = w[0] * buf[0, pl.ds(cb, tm, stride=cbs), :] + w[1] * buf[1, pl.ds(cb, tm, stride=cbs), :]
        o_ref[:, cols] = x_ref[:, cols] + _per_group(y, gate[:, cols], group, jnp.multiply)


def _combine(ys, dest, x, route, modg, layer, group):
    t, d = x.shape
    tm = DMA_ROWS
    cbs = d // LANES
    return pl.pallas_call(
        functools.partial(_combine_body, group=group),
        grid_spec=pltpu.PrefetchScalarGridSpec(
            num_scalar_prefetch=1,
            grid=(t // tm,),
            in_specs=[
                pl.BlockSpec(memory_space=pl.ANY),
                pl.BlockSpec((tm, d), lambda m, dest: (m, 0)),
                pl.BlockSpec((tm, LANES), lambda m, dest: (m, 0)),
                pl.BlockSpec((None, tm // group, d), lambda m, dest: (layer, m, 5)),
            ],
            out_specs=pl.BlockSpec((tm, d), lambda m, dest: (m, 0)),
            scratch_shapes=[pltpu.VMEM((TOP_K, tm * cbs, LANES), F32), pltpu.SemaphoreType.DMA(())],
        ),
        out_shape=jax.ShapeDtypeStruct((t, d), F32),
        compiler_params=_params("arbitrary"),
        name="combine",
    )(dest, ys, x, route, modg)


def _slot_plan(route, n_experts, n_tiles):
    ts = SLOT_TILE
    experts = jnp.arange(n_experts, dtype=jnp.int32)
    onehot = [(route[:, k].astype(jnp.int32)[:, None] == experts[None, :]).astype(jnp.int32)
              for k in range(TOP_K)]
    before = []
    offset = jnp.zeros((n_experts,), jnp.int32)
    for oh in onehot:
        cs = jnp.cumsum(oh, axis=0)
        before.append(offset[None, :] + cs - oh)
        offset = offset + cs[-1]
    padded = (offset + ts - 1) // ts * ts
    ends = jnp.cumsum(padded)
    starts = ends - padded
    dest = jnp.stack([jnp.sum(oh * (starts[None, :] + bf), axis=1) for oh, bf in zip(onehot, before)],
                     axis=1).reshape(-1)
    n_used = (ends[-1] // ts).astype(jnp.int32)
    tile_start = jnp.arange(n_tiles, dtype=jnp.int32) * ts
    tile_start = jnp.minimum(tile_start, (n_used - 1) * ts)
    tile_expert = jnp.sum((tile_start[:, None] >= ends[None, :]).astype(jnp.int32), axis=1)
    return dest, tile_expert, n_used.reshape(1)


def _moe_ffn(x, modg, layer, w_router_pad, wg, wu, wd, group):
    t, d = x.shape
    n_exp = wg.shape[0]
    cbs = d // LANES
    ts = SLOT_TILE
    n_tiles = -(-(TOP_K * t + n_exp * (ts - 1)) // ts)
    h2, route = _route(x, modg, layer, w_router_pad, group, n_exp)
    dest, tile_expert, n_used = _slot_plan(route, n_exp, n_tiles)
    hs = _dispatch(h2, dest, n_tiles * ts, cbs)
    ys = _experts(hs, tile_expert, n_used, wg, wu, wd)
    return _combine(ys, dest, x, route, modg, layer, group)


def _final_body(x_ref, gain_ref, op_ref, os_ref, *, n_prompt_tiles):
    y = _rms(x_ref[...]) * gain_ref[...]
    m = pl.program_id(0)

    @pl.when(m < n_prompt_tiles)
    def _():
        op_ref[...] = y

    @pl.when(m >= n_prompt_tiles)
    def _():
        os_ref[...] = y


def _final_norm(x, gain, t_prompt):
    t, d = x.shape
    tm = NORM_TILE
    assert t_prompt % tm == 0 and (t - t_prompt) % tm == 0
    npt = t_prompt // tm
    return pl.pallas_call(
        functools.partial(_final_body, n_prompt_tiles=npt),
        grid=(t // tm,),
        in_specs=[pl.BlockSpec((tm, d), lambda m: (m, 0)), pl.BlockSpec((1, d), lambda m: (0, 0))],
        out_specs=[pl.BlockSpec((tm, d), lambda m: (jnp.minimum(m, npt - 1), 0)),
                   pl.BlockSpec((tm, d), lambda m: (jnp.maximum(m - npt, 0), 0))],
        out_shape=[jax.ShapeDtypeStruct((t_prompt, d), F32),
                   jax.ShapeDtypeStruct((t - t_prompt, d), F32)],
        compiler_params=_params("arbitrary"),
        name="final_norm",
    )(x, gain.reshape(1, d))


def _cast_body(w_ref, o_ref):
    o_ref[...] = w_ref[...].astype(BF16)


def _cast_layer_bf16(w, layer):
    _, n_exp, rows, cols = w.shape
    tr = min(rows, 1 << ((CAST_BLOCK_BYTES // (4 * cols)).bit_length() - 1))
    assert rows % tr == 0 and tr % (2 * SUBLANES) == 0
    return pl.pallas_call(
        _cast_body,
        grid=(n_exp, rows // tr),
        in_specs=[pl.BlockSpec((None, None, tr, cols), lambda e, r: (layer, e, r, 0))],
        out_specs=pl.BlockSpec((None, tr, cols), lambda e, r: (e, r, 0)),
        out_shape=jax.ShapeDtypeStruct((n_exp, rows, cols), BF16),
        compiler_params=_params("parallel", "parallel"),
        name="cast_bf16",
    )(w)


def kernel(x_prompt, x_sample, state_hgrn, c_prompt, c_sample, w_ada, b_ada, w_in, w_out, w_spatial, b_spatial, gmlp_v_gain, gmlp_out_gain, hgrn_lb_logits, hgrn_out_gain, w_ffn_gate, w_ffn_up, w_ffn_down, w_router, w_moe_gate, w_moe_up, w_moe_down, final_gain):
    batch, seq, d = x_prompt.shape
    dec_batch, dec_seq, _ = x_sample.shape
    depth = w_in.shape[0]
    n_heads = w_spatial.shape[1]
    gmlp_chunk = w_spatial.shape[2]
    width = n_heads * LANES
    n_exp = w_router.shape[-1]
    group = dec_seq
    t_prompt = batch * seq
    t = t_prompt + dec_batch * dec_seq
    assert state_hgrn.shape[2:] == (n_heads, LANES, LANES) and gmlp_v_gain.shape[1:] == (n_heads, LANES)
    assert w_in.shape[2] == 6 * width and seq % gmlp_chunk == 0 and gmlp_chunk % HGRN_CHUNK == 0
    assert seq % group == 0 and TOKEN_TILE % group == 0 and t % TOKEN_TILE == 0 and t % DMA_ROWS == 0
    assert dec_seq <= HGRN_CHUNK and PAST_LEN % gmlp_chunk + dec_seq <= gmlp_chunk

    x = jnp.concatenate([x_prompt.reshape(t_prompt, d), x_sample.reshape(-1, d)], axis=0)
    c = jnp.concatenate([c_prompt, c_sample], axis=0)
    mod = _ada_modulation(c, w_ada, b_ada)
    mod_prompt = jnp.broadcast_to(mod[:, :batch, None, :], (depth, batch, seq // group, mod.shape[-1]))
    modg = jnp.concatenate([mod_prompt.reshape(depth, t_prompt // group, -1), mod[:, batch:]], axis=1)

    p = jax.nn.softmax(hgrn_lb_logits.astype(F32), axis=0)
    lbs = jnp.cumsum(p, axis=0) - p[0]
    w_router_pad = jnp.pad(w_router, ((0, 0), (0, 0), (0, LANES - n_exp)))
    b_sp_t = jnp.swapaxes(b_spatial, 1, 2)

    states_p, states_s, v_rows = [], [], []
    for layer in range(depth):
        proj = _in_projection(x, modg, layer, w_in[layer].astype(BF16), group)
        small = (w_spatial[layer], b_sp_t[layer], gmlp_v_gain[layer].reshape(1, width),
                 gmlp_out_gain[layer].reshape(1, width), lbs[layer].reshape(1, width),
                 hgrn_out_gain[layer].reshape(1, width))
        y_p, s_p = _token_mixer(proj, None, *small, n_streams=batch, seq=seq, row0=0,
                                tb=gmlp_chunk, chunk=HGRN_CHUNK, n_heads=n_heads)
        y_s, s_s, v_s = _token_mixer(proj, state_hgrn[layer], *small, n_streams=dec_batch,
                                     seq=dec_seq, row0=t_prompt, tb=dec_seq, chunk=dec_seq,
                                     n_heads=n_heads)
        ymix = jnp.concatenate([y_p, y_s], axis=0)
        x = _out_projection(ymix, w_out[layer].astype(BF16), x, modg, layer, group)
        j = layer // 2
        if layer % 2 == 0:
            x = _dense_ffn(x, modg, layer, w_ffn_gate[j].astype(BF16), w_ffn_up[j].astype(BF16),
                           w_ffn_down[j].astype(BF16), group)
        else:
            x = _moe_ffn(x, modg, layer, w_router_pad[j], _cast_layer_bf16(w_moe_gate, j),
                         _cast_layer_bf16(w_moe_up, j), _cast_layer_bf16(w_moe_down, j), group)
        states_p.append(s_p)
        states_s.append(s_s)
        v_rows.append(v_s.reshape(dec_batch, dec_seq, width))

    y_prompt, y_sample = _final_norm(x, final_gain, t_prompt)
    return (y_prompt.reshape(batch, seq, d), y_sample.reshape(dec_batch, dec_seq, d),
            jnp.stack(states_p), jnp.stack(states_s), jnp.stack(v_rows))
```

```python
import functools

import jax
import jax.numpy as jnp
from jax import lax
from jax.experimental import pallas as pl
from jax.experimental.pallas import tpu as pltpu

F32 = jnp.float32
BF16 = jnp.bfloat16

EPS = 1e-6
EXP_CLIP = 60.0
HGRN_CHUNK = 64
PAST_LEN = 2048
TOP_K = 2

LANES = 128
SUBLANES = 8
VMEM_LIMIT_V7X = 56 * 1024 * 1024

TOKEN_TILE = 768
SLOT_TILE = 768
NORM_TILE = 512
PROLOGUE_ROWS = 256
CAST_BLOCK_BYTES = 8 * 1024 * 1024
FF_TILE = 512
IN_TILE = 1024
OUT_TILE = 1024
ADA_TILE = 1024
DMA_ROWS = 512
DMA_UNROLL = 8


def _params(*semantics):
    return pltpu.CompilerParams(dimension_semantics=semantics, vmem_limit_bytes=VMEM_LIMIT_V7X)


def _silu(x):
    return x * (1.0 / (1.0 + jnp.exp(-x)))


def _rms(x):
    return x * lax.rsqrt(jnp.mean(x * x, axis=-1, keepdims=True) + EPS)


def _per_group(y, m, group, op):
    r, c = y.shape
    return op(y.reshape(r // group, group, c), m[:, None, :]).reshape(r, c)


def _norm_mod(x, sc, sh, group):
    y = _per_group(_rms(x), 1.0 + sc, group, jnp.multiply)
    return _per_group(y, sh, group, jnp.add)


def _row_chunks(n_rows, group):
    size = PROLOGUE_ROWS if n_rows % PROLOGUE_ROWS == 0 else n_rows
    assert size % (group * SUBLANES) == 0 or size == n_rows
    return [(r, size) for r in range(0, n_rows, size)]


def _norm_mod_rows(x_ref, sc_ref, sh_ref, group, r0, nr):
    g0, ng = r0 // group, nr // group
    return _norm_mod(x_ref[pl.ds(r0, nr), :], sc_ref[pl.ds(g0, ng), :], sh_ref[pl.ds(g0, ng), :], group)


def _dot(a, b):
    return jnp.dot(a, b, preferred_element_type=F32)


def _dot_nt(a, b):
    return lax.dot_general(a, b, (((1,), (1,)), ((), ())), preferred_element_type=F32)


def _dot_tn(a, b):
    return lax.dot_general(a, b, (((0,), (0,)), ((), ())), preferred_element_type=F32)


def _split3(x):
    hi = x.astype(BF16)
    r = x - hi.astype(F32)
    mid = r.astype(BF16)
    lo = (r - mid.astype(F32)).astype(BF16)
    return hi, mid, lo


def _dot_f32(a, b):
    a1, a2, a3 = _split3(a)
    b1, b2, b3 = b if isinstance(b, tuple) else _split3(b)
    return (_dot(a1, b1) + (_dot(a1, b2) + _dot(a2, b1))
            + (_dot(a2, b2) + _dot(a1, b3) + _dot(a3, b1)))


def _ada_body(c_ref, w_ref, b_ref, o_ref):
    o_ref[...] = _dot_f32(_silu(c_ref[...]), w_ref[...]) + b_ref[...]


def _ada_modulation(c, w_ada, b_ada):
    depth, d, n = w_ada.shape
    n_streams = c.shape[0]
    s = -(-n_streams // (2 * SUBLANES)) * 2 * SUBLANES
    c = jnp.pad(c, ((0, s - n_streams), (0, 0)))
    tn = min(ADA_TILE, n)
    mod = pl.pallas_call(
        _ada_body,
        grid=(depth, n // tn),
        in_specs=[
            pl.BlockSpec((s, d), lambda l, j: (0, 0)),
            pl.BlockSpec((None, d, tn), lambda l, j: (l, 0, j)),
            pl.BlockSpec((None, 1, tn), lambda l, j: (l, 0, j)),
        ],
        out_specs=pl.BlockSpec((None, s, tn), lambda l, j: (l, 0, j)),
        out_shape=jax.ShapeDtypeStruct((depth, s, n), F32),
        compiler_params=_params("parallel", "parallel"),
        name="ada_modulation",
    )(c, w_ada, b_ada.reshape(depth, 1, n))
    return mod[:, :n_streams]


def _inproj_body(x_ref, sc_ref, sh_ref, w_ref, o_ref, h_scr, *, group):
    @pl.when(pl.program_id(1) == 0)
    def _():
        for r0, nr in _row_chunks(x_ref.shape[0], group):
            h_scr[pl.ds(r0, nr), :] = _norm_mod_rows(x_ref, sc_ref, sh_ref, group, r0, nr).astype(BF16)

    o_ref[...] = _dot(h_scr[...], w_ref[...])


def _in_projection(x, modg, layer, w, group):
    t, d = x.shape
    n = w.shape[1]
    tm, tn = TOKEN_TILE, min(IN_TILE, n)
    gm = tm // group
    return pl.pallas_call(
        functools.partial(_inproj_body, group=group),
        grid=(t // tm, n // tn),
        in_specs=[
            pl.BlockSpec((tm, d), lambda m, j: (m, 0)),
            pl.BlockSpec((None, gm, d), lambda m, j: (layer, m, 1)),
            pl.BlockSpec((None, gm, d), lambda m, j: (layer, m, 0)),
            pl.BlockSpec((d, tn), lambda m, j: (0, j)),
        ],
        out_specs=pl.BlockSpec((tm, tn), lambda m, j: (m, j)),
        out_shape=jax.ShapeDtypeStruct((t, n), F32),
        scratch_shapes=[pltpu.VMEM((tm, d), BF16)],
        compiler_params=_params("parallel", "arbitrary"),
        name="in_projection",
    )(x, modg, modg, w)


def _mixer_body(*refs, tb, chunk, n_heads, pos0, has_state0, emit_v):
    (u_ref, v_ref, q_ref, f_ref, i_ref, g_ref, wsp_ref, bsp_ref, vgain_ref, again_ref,
     lb_ref, ogain_ref) = refs[:12]
    refs = refs[12:]
    if has_state0:
        state0_ref, refs = refs[0], refs[1:]
    ymix_ref, state_ref = refs[:2]
    refs = refs[2:]
    if emit_v:
        vout_ref, refs = refs[0], refs[1:]
    (s_scr,) = refs

    j = pl.program_id(1)
    width = n_heads * LANES

    @pl.when(j == 0)
    def _():
        if has_state0:
            s_scr[...] = state0_ref[...]
        else:
            s_scr[...] = jnp.zeros_like(s_scr)

    def head(h):
        return slice(h * LANES, (h + 1) * LANES)

    row = lax.broadcasted_iota(jnp.int32, (tb, tb), 0)
    col = lax.broadcasted_iota(jnp.int32, (tb, tb), 1)
    v = v_ref[...]
    u = u_ref[...]
    vgain = vgain_ref[...]
    gated = []
    sq = jnp.zeros((tb, 1), F32)
    for h in range(n_heads):
        vn = _rms(v[:, head(h)]) * vgain[:, head(h)]
        if emit_v:
            vout_ref[:, head(h)] = vn
        w = jnp.where(row >= col, wsp_ref[h, pl.ds(pos0, tb), pl.ds(pos0, tb)], 0.0)
        mixed = _dot(w.astype(BF16), vn.astype(BF16)) + bsp_ref[:, h:h + 1]
        ga = u[:, head(h)] * mixed
        sq = sq + jnp.sum(ga * ga, axis=-1, keepdims=True)
        gated.append(ga)
    inv = lax.rsqrt(sq / width + EPS)
    again = again_ref[...]
    for h in range(n_heads):
        ymix_ref[:, head(h)] = (gated[h] * inv * again[:, head(h)]).astype(BF16)

    lb = lb_ref[...]
    ogain = ogain_ref[...]
    small = [m for m in (4, 2, 1) if m < SUBLANES]
    n_sel = (1 + len(small)) * chunk
    sel_row = lax.broadcasted_iota(jnp.int32, (n_sel, 3 * chunk), 0)
    sel_col = lax.broadcasted_iota(jnp.int32, (n_sel, 3 * chunk), 1) % chunk
    pos = sel_row % chunk
    bound = pos
    for n, m in enumerate(small):
        bound = jnp.where(sel_row // chunk == n + 1, pos - pos % (2 * m) + m - 1, bound)
    sel = (sel_col <= bound).astype(BF16)
    trow = lax.broadcasted_iota(jnp.int32, (chunk, width), 0)
    upper = {m: trow % (2 * m) >= m for m in small}
    ti = lax.broadcasted_iota(jnp.int32, (chunk, chunk), 0)
    si = lax.broadcasted_iota(jnp.int32, (chunk, chunk), 1)
    same_block = {m: ti - ti % (2 * m) == si - si % (2 * m) for m in small}
    diag = ti == si
    eye = (lax.broadcasted_iota(jnp.int32, (LANES, LANES), 0)
           == lax.broadcasted_iota(jnp.int32, (LANES, LANES), 1))
    for c in range(tb // chunk):
        rows = pl.ds(c * chunk, chunk)
        fx = f_ref[rows, :]
        e = jnp.exp(-jnp.abs(fx))
        log_f = (jnp.minimum(fx, 0.0) - jnp.log1p(e)
                 + jnp.log1p(lb * jnp.exp(jnp.minimum(-fx, EXP_CLIP))))
        kx = (1.0 - lb) * (jnp.where(fx >= 0.0, e, 1.0) / (1.0 + e))
        qs = _silu(q_ref[rows, :])
        vx = i_ref[rows, :]
        sums = _dot(sel, jnp.concatenate(_split3(log_f), axis=0))
        cum = sums[:chunk]

        last = cum[chunk - 1:chunk, :]
        qe = (qs * jnp.exp(cum)).astype(BF16)
        kd = (kx * jnp.exp(last - cum)).astype(BF16)
        el = jnp.exp(last)
        vb = vx.astype(BF16)
        qb = qs.astype(BF16)
        kb = kx.astype(BF16)
        a_small, b_small = [], []
        for n, m in enumerate(small):
            ref = sums[(n + 1) * chunk:(n + 2) * chunk]
            a_small.append(jnp.where(upper[m], qs * jnp.exp(jnp.minimum(cum - ref, 0.0)), 0.0).astype(BF16))
            b_small.append(jnp.where(upper[m], 0.0, kx * jnp.exp(jnp.minimum(ref - cum, 0.0))).astype(BF16))

        scores = [jnp.where(diag, _dot_nt(qb[:, head(h)], kb[:, head(h)]), 0.0) for h in range(n_heads)]
        for n, m in enumerate(small):
            part = [_dot_nt(a_small[n][:, head(h)], b_small[n][:, head(h)]) for h in range(n_heads)]
            scores = [s + jnp.where(same_block[m], p, 0.0) for s, p in zip(scores, part)]
        s_old = [s_scr[h] for h in range(n_heads)]
        read = [_dot(qe[:, head(h)], s_old[h].astype(BF16)) for h in range(n_heads)]
        grow = [_dot_tn(kd[:, head(h)], vb[:, head(h)]) for h in range(n_heads)]
        out = [_dot(scores[h].astype(BF16), vb[:, head(h)]) + read[h] for h in range(n_heads)]
        for h in range(n_heads):
            ecol = jnp.sum(jnp.where(eye, jnp.broadcast_to(el[:, head(h)], (LANES, LANES)), 0.0),
                           axis=1, keepdims=True)
            s_scr[h] = s_old[h] * ecol + grow[h]

        m = chunk // 2
        while m >= SUBLANES:
            blocks = [(b * 2 * m, b * 2 * m + m) for b in range(chunk // (2 * m))]
            sc = []
            for lo, mid in blocks:
                cref = cum[mid - 1:mid, :]
                a_blk = (qs[mid:mid + m, :] * jnp.exp(cum[mid:mid + m, :] - cref)).astype(BF16)
                b_blk = (kx[lo:mid, :] * jnp.exp(cref - cum[lo:mid, :])).astype(BF16)
                sc.append([_dot_nt(a_blk[:, head(h)], b_blk[:, head(h)]) for h in range(n_heads)])
            zero = jnp.zeros((m, LANES), F32)
            for h in range(n_heads):
                pieces = []
                for (lo, mid), s_blk in zip(blocks, sc):
                    pieces += [zero, _dot(s_blk[h].astype(BF16), vb[lo:mid, head(h)])]
                out[h] = out[h] + jnp.concatenate(pieces, axis=0)
            m //= 2

        gx = _silu(g_ref[rows, :])
        for h in range(n_heads):
            yb = _rms(out[h]) * ogain[:, head(h)] * gx[:, head(h)]
            ymix_ref[rows, pl.ds(width + h * LANES, LANES)] = yb.astype(BF16)

    @pl.when(j == pl.num_programs(1) - 1)
    def _():
        state_ref[...] = s_scr[...]


def _token_mixer(proj, state0, w_sp, b_sp_t, v_gain, a_gain, lb, o_gain, *,
                 n_streams, seq, row0, tb, chunk, n_heads):
    width = n_heads * LANES
    nblk = seq // tb
    blk0 = row0 // tb
    has_state0 = state0 is not None
    emit_v = has_state0
    pos0 = PAST_LEN % w_sp.shape[-1] if has_state0 else 0
    assert pos0 % tb == 0

    def rows_map(col):
        return lambda s, j: (blk0 + s * nblk + j, col)

    const2 = lambda s, j: (0, 0)
    in_specs = [pl.BlockSpec((tb, width), rows_map(k)) for k in range(6)]
    in_specs += [
        pl.BlockSpec(w_sp.shape, lambda s, j: (0, 0, 0)),
        pl.BlockSpec((tb, n_heads), lambda s, j: (pos0 // tb, 0)),
        pl.BlockSpec((1, width), const2),
        pl.BlockSpec((1, width), const2),
        pl.BlockSpec((1, width), const2),
        pl.BlockSpec((1, width), const2),
    ]
    args = [proj] * 6 + [w_sp, b_sp_t, v_gain, a_gain, lb, o_gain]
    if has_state0:
        in_specs.append(pl.BlockSpec((None, n_heads, LANES, LANES), lambda s, j: (s, 0, 0, 0)))
        args.append(state0)
    own_rows = lambda s, j: (s * nblk + j, 0)
    out_shape = [jax.ShapeDtypeStruct((n_streams * seq, 2 * width), BF16),
                 jax.ShapeDtypeStruct((n_streams, n_heads, LANES, LANES), F32)]
    out_specs = [pl.BlockSpec((tb, 2 * width), own_rows),
                 pl.BlockSpec((None, n_heads, LANES, LANES), lambda s, j: (s, 0, 0, 0))]
    if emit_v:
        out_shape.append(jax.ShapeDtypeStruct((n_streams * seq, width), F32))
        out_specs.append(pl.BlockSpec((tb, width), own_rows))

    return pl.pallas_call(
        functools.partial(_mixer_body, tb=tb, chunk=chunk, n_heads=n_heads, pos0=pos0,
                          has_state0=has_state0, emit_v=emit_v),
        grid=(n_streams, nblk),
        in_specs=in_specs,
        out_specs=out_specs,
        out_shape=out_shape,
        scratch_shapes=[
            pltpu.VMEM((n_heads, LANES, LANES), F32),
        ],
        compiler_params=_params("parallel", "arbitrary"),
        name="token_mixer_sample" if has_state0 else "token_mixer_prompt",
    )(*args)


def _outproj_body(y_ref, w_ref, x_ref, g_ref, o_ref, *, group):
    y = _dot(y_ref[...], w_ref[...])
    o_ref[...] = x_ref[...] + _per_group(y, g_ref[...], group, jnp.multiply)


def _out_projection(ymix, w, x, modg, layer, group):
    t, k = ymix.shape
    d = w.shape[1]
    tm, tn = TOKEN_TILE, min(OUT_TILE, d)
    ncol = d // tn
    return pl.pallas_call(
        functools.partial(_outproj_body, group=group),
        grid=(t // tm, ncol),
        in_specs=[
            pl.BlockSpec((tm, k), lambda m, j: (m, 0)),
            pl.BlockSpec((k, tn), lambda m, j: (0, j)),
            pl.BlockSpec((tm, tn), lambda m, j: (m, j)),
            pl.BlockSpec((None, tm // group, tn), lambda m, j: (layer, m, 2 * ncol + j)),
        ],
        out_specs=pl.BlockSpec((tm, tn), lambda m, j: (m, j)),
        out_shape=jax.ShapeDtypeStruct((t, d), F32),
        compiler_params=_params("parallel", "arbitrary"),
        name="out_projection",
    )(ymix, w, x, modg)


def _ffn_body(x_ref, sc_ref, sh_ref, wg_ref, wu_ref, wd_ref, gate_ref, o_ref, h_scr, *, group):
    f = pl.program_id(1)

    @pl.when(f == 0)
    def _():
        for r0, nr in _row_chunks(x_ref.shape[0], group):
            h_scr[pl.ds(r0, nr), :] = _norm_mod_rows(x_ref, sc_ref, sh_ref, group, r0, nr).astype(BF16)
        o_ref[...] = jnp.zeros_like(o_ref)

    h = h_scr[...]
    a = _silu(_dot(h, wg_ref[...])) * _dot(h, wu_ref[...])
    o_ref[...] += _dot(a.astype(BF16), wd_ref[...])

    @pl.when(f == pl.num_programs(1) - 1)
    def _():
        for r0, nr in _row_chunks(x_ref.shape[0], group):
            rows, grows = pl.ds(r0, nr), pl.ds(r0 // group, nr // group)
            o_ref[rows, :] = x_ref[rows, :] + _per_group(o_ref[rows, :], gate_ref[grows, :], group,
                                                         jnp.multiply)


def _dense_ffn(x, modg, layer, wg, wu, wd, group):
    t, d = x.shape
    ff = wg.shape[1]
    tm, tf = TOKEN_TILE, FF_TILE
    gm = tm // group
    return pl.pallas_call(
        functools.partial(_ffn_body, group=group),
        grid=(t // tm, ff // tf),
        in_specs=[
            pl.BlockSpec((tm, d), lambda m, f: (m, 0)),
            pl.BlockSpec((None, gm, d), lambda m, f: (layer, m, 4)),
            pl.BlockSpec((None, gm, d), lambda m, f: (layer, m, 3)),
            pl.BlockSpec((d, tf), lambda m, f: (0, f)),
            pl.BlockSpec((d, tf), lambda m, f: (0, f)),
            pl.BlockSpec((tf, d), lambda m, f: (f, 0)),
            pl.BlockSpec((None, gm, d), lambda m, f: (layer, m, 5)),
        ],
        out_specs=pl.BlockSpec((tm, d), lambda m, f: (m, 0)),
        out_shape=jax.ShapeDtypeStruct((t, d), F32),
        scratch_shapes=[pltpu.VMEM((tm, d), BF16)],
        compiler_params=_params("parallel", "arbitrary"),
        name="dense_swiglu",
    )(x, modg, modg, wg, wu, wd, modg)


def _route_body(x_ref, sc_ref, sh_ref, wr_ref, h_ref, route_ref, *, group, n_experts):
    tm, d = x_ref.shape
    cbs = d // LANES
    wr = _split3(wr_ref[...])
    for r0, nr in _row_chunks(tm, group):
        h = _norm_mod_rows(x_ref, sc_ref, sh_ref, group, r0, nr)
        hf = h.astype(BF16).astype(F32)
        for cb in range(cbs):
            h_ref[pl.ds(r0 * cbs + cb, nr, stride=cbs), :] = hf[:, cb * LANES:(cb + 1) * LANES]
        logits = _dot_f32(h, wr)
        lane = lax.broadcasted_iota(jnp.int32, logits.shape, 1)
        neg = jnp.float32(-jnp.inf)
        l1 = jnp.where(lane < n_experts, logits, neg)
        m1 = jnp.max(l1, axis=-1, keepdims=True)
        i1 = jnp.min(jnp.where(l1 == m1, lane, LANES), axis=-1, keepdims=True)
        l2 = jnp.where(lane == i1, neg, l1)
        m2 = jnp.max(l2, axis=-1, keepdims=True)
        i2 = jnp.min(jnp.where(l2 == m2, lane, LANES), axis=-1, keepdims=True)
        p = jnp.exp(m2 - m1)
        w1 = 1.0 / (1.0 + p)
        w2 = p / (1.0 + p)
        route_ref[pl.ds(r0, nr), :] = jnp.where(
            lane == 0, i1.astype(F32),
            jnp.where(lane == 1, i2.astype(F32), jnp.where(lane == 2, w1, jnp.where(lane == 3, w2, 0.0))))


def _route(x, modg, layer, w_router_pad, group, n_experts):
    t, d = x.shape
    tm = TOKEN_TILE
    gm = tm // group
    cbs = d // LANES
    return pl.pallas_call(
        functools.partial(_route_body, group=group, n_experts=n_experts),
        grid=(t // tm,),
        in_specs=[
            pl.BlockSpec((tm, d), lambda m: (m, 0)),
            pl.BlockSpec((None, gm, d), lambda m: (layer, m, 4)),
            pl.BlockSpec((None, gm, d), lambda m: (layer, m, 3)),
            pl.BlockSpec((d, LANES), lambda m: (0, 0)),
        ],
        out_specs=[pl.BlockSpec((tm * cbs, LANES), lambda m: (m, 0)),
                   pl.BlockSpec((tm, LANES), lambda m: (m, 0))],
        out_shape=[jax.ShapeDtypeStruct((t * cbs, LANES), F32),
                   jax.ShapeDtypeStruct((t, LANES), F32)],
        compiler_params=_params("parallel"),
        name="route",
    )(x, modg, modg, w_router_pad)


def _token_copy(src, dst, sem, src_tok, dst_tok, cbs, n=1):
    s0 = pl.multiple_of(src_tok * cbs, cbs)
    d0 = pl.multiple_of(dst_tok * cbs, cbs)
    return pltpu.make_async_copy(src.at[pl.ds(s0, n * cbs)], dst.at[pl.ds(d0, n * cbs)], sem)


def _dispatch_body(dest_ref, h_ref, init_ref, hs_ref, sem, *, rows, cbs):
    del init_ref
    base = pl.program_id(0) * rows

    def issue(r, carry):
        for k in range(TOP_K):
            _token_copy(h_ref, hs_ref, sem, r, dest_ref[TOP_K * (base + r) + k], cbs).start()
        return carry

    lax.fori_loop(0, rows, issue, 0, unroll=DMA_UNROLL)
    for k in range(TOP_K):
        _token_copy(h_ref, hs_ref, sem, 0, 0, cbs, rows).wait()


def _dispatch(h2, dest, n_slots, cbs):
    t = h2.shape[0] // cbs
    rows = DMA_ROWS
    hs0 = jnp.zeros((n_slots * cbs, LANES), h2.dtype)
    return pl.pallas_call(
        functools.partial(_dispatch_body, rows=rows, cbs=cbs),
        grid_spec=pltpu.PrefetchScalarGridSpec(
            num_scalar_prefetch=1,
            grid=(t // rows,),
            in_specs=[pl.BlockSpec((rows * cbs, LANES), lambda m, dest: (m, 0)),
                      pl.BlockSpec(memory_space=pl.ANY)],
            out_specs=pl.BlockSpec(memory_space=pl.ANY),
            scratch_shapes=[pltpu.SemaphoreType.DMA(())],
        ),
        out_shape=jax.ShapeDtypeStruct(hs0.shape, hs0.dtype),
        input_output_aliases={2: 0},
        compiler_params=_params("arbitrary"),
        name="dispatch",
    )(dest, h2, hs0)


def _experts_body(te_ref, nu_ref, hs_ref, wg_ref, wu_ref, wd_ref, o_ref, h_scr, acc_scr):
    del te_ref
    i, f = pl.program_id(0), pl.program_id(1)
    nf = pl.num_programs(1)
    ts, d = h_scr.shape
    cbs = d // LANES
    used = i < nu_ref[0]

    @pl.when(used & (f == 0))
    def _():
        for cb in range(cbs):
            h_scr[:, cb * LANES:(cb + 1) * LANES] = hs_ref[pl.ds(cb, ts, stride=cbs), :].astype(BF16)
        acc_scr[...] = jnp.zeros_like(acc_scr)

    @pl.when(used)
    def _():
        h = h_scr[...]
        a = _silu(_dot(h, wg_ref[...])) * _dot(h, wu_ref[...])
        acc_scr[...] += _dot(a.astype(BF16), wd_ref[...])

    @pl.when(used & (f == nf - 1))
    def _():
        for cb in range(cbs):
            o_ref[pl.ds(cb, ts, stride=cbs), :] = acc_scr[:, cb * LANES:(cb + 1) * LANES]

    @pl.when(jnp.logical_not(used) & (f == nf - 1))
    def _():
        o_ref[...] = jnp.zeros_like(o_ref)


def _experts(hs, tile_expert, n_used, wg, wu, wd):
    n_exp, d, ff = wg.shape
    cbs = d // LANES
    n_slots = hs.shape[0] // cbs
    ts, tf = SLOT_TILE, FF_TILE
    nf = ff // tf

    def fcol(i, f, nu):
        return jnp.where(i < nu[0], f, nf - 1)

    return pl.pallas_call(
        _experts_body,
        grid_spec=pltpu.PrefetchScalarGridSpec(
            num_scalar_prefetch=2,
            grid=(n_slots // ts, nf),
            in_specs=[
                pl.BlockSpec((ts * cbs, LANES), lambda i, f, te, nu: (i, 0)),
                pl.BlockSpec((None, d, tf), lambda i, f, te, nu: (te[i], 0, fcol(i, f, nu))),
                pl.BlockSpec((None, d, tf), lambda i, f, te, nu: (te[i], 0, fcol(i, f, nu))),
                pl.BlockSpec((None, tf, d), lambda i, f, te, nu: (te[i], fcol(i, f, nu), 0)),
            ],
            out_specs=pl.BlockSpec((ts * cbs, LANES), lambda i, f, te, nu: (i, 0)),
            scratch_shapes=[pltpu.VMEM((ts, d), BF16), pltpu.VMEM((ts, d), F32)],
        ),
        out_shape=jax.ShapeDtypeStruct((n_slots * cbs, LANES), F32),
        compiler_params=_params("parallel", "arbitrary"),
        name="experts",
    )(tile_expert, n_used, hs, wg, wu, wd)


def _combine_body(dest_ref, ys_ref, x_ref, route_ref, gate_ref, o_ref, buf, sem, *, group):
    tm, d = x_ref.shape
    cbs = d // LANES
    base = pl.program_id(0) * tm

    def issue(r, carry):
        for k in range(TOP_K):
            _token_copy(ys_ref, buf.at[k], sem, dest_ref[TOP_K * (base + r) + k], r, cbs).start()
        return carry

    lax.fori_loop(0, tm, issue, 0, unroll=DMA_UNROLL)
    for k in range(TOP_K):
        _token_copy(ys_ref, buf.at[k], sem, 0, 0, cbs, tm).wait()
    w = [route_ref[:, TOP_K + k:TOP_K + k + 1] for k in range(TOP_K)]
    gate = gate_ref[...]
    for cb in range(cbs):
        cols = slice(cb * LANES, (cb + 1) * LANES)
        y = w[0] * buf[0, pl.ds(cb, tm, stride=cbs), :] + w[1] * buf[1, pl.ds(cb, tm, stride=cbs), :]
        o_ref[:, cols] = x_ref[:, cols] + _per_group(y, gate[:, cols], group, jnp.multiply)


def _combine(ys, dest, x, route, modg, layer, group):
    t, d = x.shape
    tm = DMA_ROWS
    cbs = d // LANES
    return pl.pallas_call(
        functools.partial(_combine_body, group=group),
        grid_spec=pltpu.PrefetchScalarGridSpec(
            num_scalar_prefetch=1,
            grid=(t // tm,),
            in_specs=[
                pl.BlockSpec(memory_space=pl.ANY),
                pl.BlockSpec((tm, d), lambda m, dest: (m, 0)),
                pl.BlockSpec((tm, LANES), lambda m, dest: (m, 0)),
                pl.BlockSpec((None, tm // group, d), lambda m, dest: (layer, m, 5)),
            ],
            out_specs=pl.BlockSpec((tm, d), lambda m, dest: (m, 0)),
            scratch_shapes=[pltpu.VMEM((TOP_K, tm * cbs, LANES), F32), pltpu.SemaphoreType.DMA(())],
        ),
        out_shape=jax.ShapeDtypeStruct((t, d), F32),
        compiler_params=_params("arbitrary"),
        name="combine",
    )(dest, ys, x, route, modg)


def _slot_plan(route, n_experts, n_tiles):
    ts = SLOT_TILE
    experts = jnp.arange(n_experts, dtype=jnp.int32)
    onehot = [(route[:, k].astype(jnp.int32)[:, None] == experts[None, :]).astype(jnp.int32)
              for k in range(TOP_K)]
    before = []
    offset = jnp.zeros((n_experts,), jnp.int32)
    for oh in onehot:
        cs = jnp.cumsum(oh, axis=0)
        before.append(offset[None, :] + cs - oh)
        offset = offset + cs[-1]
    padded = (offset + ts - 1) // ts * ts
    ends = jnp.cumsum(padded)
    starts = ends - padded
    dest = jnp.stack([jnp.sum(oh * (starts[None, :] + bf), axis=1) for oh, bf in zip(onehot, before)],
                     axis=1).reshape(-1)
    n_used = (ends[-1] // ts).astype(jnp.int32)
    tile_start = jnp.arange(n_tiles, dtype=jnp.int32) * ts
    tile_start = jnp.minimum(tile_start, (n_used - 1) * ts)
    tile_expert = jnp.sum((tile_start[:, None] >= ends[None, :]).astype(jnp.int32), axis=1)
    return dest, tile_expert, n_used.reshape(1)


def _moe_ffn(x, modg, layer, w_router_pad, wg, wu, wd, group):
    t, d = x.shape
    n_exp = wg.shape[0]
    cbs = d // LANES
    ts = SLOT_TILE
    n_tiles = -(-(TOP_K * t + n_exp * (ts - 1)) // ts)
    h2, route = _route(x, modg, layer, w_router_pad, group, n_exp)
    dest, tile_expert, n_used = _slot_plan(route, n_exp, n_tiles)
    hs = _dispatch(h2, dest, n_tiles * ts, cbs)
    ys = _experts(hs, tile_expert, n_used, wg, wu, wd)
    return _combine(ys, dest, x, route, modg, layer, group)


def _final_body(x_ref, gain_ref, op_ref, os_ref, *, n_prompt_tiles):
    y = _rms(x_ref[...]) * gain_ref[...]
    m = pl.program_id(0)

    @pl.when(m < n_prompt_tiles)
    def _():
        op_ref[...] = y

    @pl.when(m >= n_prompt_tiles)
    def _():
        os_ref[...] = y


def _final_norm(x, gain, t_prompt):
    t, d = x.shape
    tm = NORM_TILE
    assert t_prompt % tm == 0 and (t - t_prompt) % tm == 0
    npt = t_prompt // tm
    return pl.pallas_call(
        functools.partial(_final_body, n_prompt_tiles=npt),
        grid=(t // tm,),
        in_specs=[pl.BlockSpec((tm, d), lambda m: (m, 0)), pl.BlockSpec((1, d), lambda m: (0, 0))],
        out_specs=[pl.BlockSpec((tm, d), lambda m: (jnp.minimum(m, npt - 1), 0)),
                   pl.BlockSpec((tm, d), lambda m: (jnp.maximum(m - npt, 0), 0))],
        out_shape=[jax.ShapeDtypeStruct((t_prompt, d), F32),
                   jax.ShapeDtypeStruct((t - t_prompt, d), F32)],
        compiler_params=_params("arbitrary"),
        name="final_norm",
    )(x, gain.reshape(1, d))


def _cast_body(w_ref, o_ref):
    o_ref[...] = w_ref[...].astype(BF16)


def _cast_layer_bf16(w, layer):
    _, n_exp, rows, cols = w.shape
    tr = min(rows, 1 << ((CAST_BLOCK_BYTES // (4 * cols)).bit_length() - 1))
    assert rows % tr == 0 and tr % (2 * SUBLANES) == 0
    return pl.pallas_call(
        _cast_body,
        grid=(n_exp, rows // tr),
        in_specs=[pl.BlockSpec((None, None, tr, cols), lambda e, r: (layer, e, r, 0))],
        out_specs=pl.BlockSpec((None, tr, cols), lambda e, r: (e, r, 0)),
        out_shape=jax.ShapeDtypeStruct((n_exp, rows, cols), BF16),
        compiler_params=_params("parallel", "parallel"),
        name="cast_bf16",
    )(w)


def kernel(x_prompt, x_sample, state_hgrn, c_prompt, c_sample, w_ada, b_ada, w_in, w_out, w_spatial, b_spatial, gmlp_v_gain, gmlp_out_gain, hgrn_lb_logits, hgrn_out_gain, w_ffn_gate, w_ffn_up, w_ffn_down, w_router, w_moe_gate, w_moe_up, w_moe_down, final_gain):
    batch, seq, d = x_prompt.shape
    dec_batch, dec_seq, _ = x_sample.shape
    depth = w_in.shape[0]
    n_heads = w_spatial.shape[1]
    gmlp_chunk = w_spatial.shape[2]
    width = n_heads * LANES
    n_exp = w_router.shape[-1]
    group = dec_seq
    t_prompt = batch * seq
    t = t_prompt + dec_batch * dec_seq
    assert state_hgrn.shape[2:] == (n_heads, LANES, LANES) and gmlp_v_gain.shape[1:] == (n_heads, LANES)
    assert w_in.shape[2] == 6 * width and seq % gmlp_chunk == 0 and gmlp_chunk % HGRN_CHUNK == 0
    assert seq % group == 0 and TOKEN_TILE % group == 0 and t % TOKEN_TILE == 0 and t % DMA_ROWS == 0
    assert dec_seq <= HGRN_CHUNK and PAST_LEN % gmlp_chunk + dec_seq <= gmlp_chunk

    x = jnp.concatenate([x_prompt.reshape(t_prompt, d), x_sample.reshape(-1, d)], axis=0)
    c = jnp.concatenate([c_prompt, c_sample], axis=0)
    mod = _ada_modulation(c, w_ada, b_ada)
    mod_prompt = jnp.broadcast_to(mod[:, :batch, None, :], (depth, batch, seq // group, mod.shape[-1]))
    modg = jnp.concatenate([mod_prompt.reshape(depth, t_prompt // group, -1), mod[:, batch:]], axis=1)

    p = jax.nn.softmax(hgrn_lb_logits.astype(F32), axis=0)
    lbs = jnp.cumsum(p, axis=0) - p[0]
    w_router_pad = jnp.pad(w_router, ((0, 0), (0, 0), (0, LANES - n_exp)))
    b_sp_t = jnp.swapaxes(b_spatial, 1, 2)

    states_p, states_s, v_rows = [], [], []
    for layer in range(depth):
        proj = _in_projection(x, modg, layer, w_in[layer].astype(BF16), group)
        small = (w_spatial[layer], b_sp_t[layer], gmlp_v_gain[layer].reshape(1, width),
                 gmlp_out_gain[layer].reshape(1, width), lbs[layer].reshape(1, width),
                 hgrn_out_gain[layer].reshape(1, width))
        y_p, s_p = _token_mixer(proj, None, *small, n_streams=batch, seq=seq, row0=0,
                                tb=gmlp_chunk, chunk=HGRN_CHUNK, n_heads=n_heads)
        y_s, s_s, v_s = _token_mixer(proj, state_hgrn[layer], *small, n_streams=dec_batch,
                                     seq=dec_seq, row0=t_prompt, tb=dec_seq, chunk=dec_seq,
                                     n_heads=n_heads)
        ymix = jnp.concatenate([y_p, y_s], axis=0)
        x = _out_projection(ymix, w_out[layer].astype(BF16), x, modg, layer, group)
        j = layer // 2
        if layer % 2 == 0:
            x = _dense_ffn(x, modg, layer, w_ffn_gate[j].astype(BF16), w_ffn_up[j].astype(BF16),
                           w_ffn_down[j].astype(BF16), group)
        else:
            x = _moe_ffn(x, modg, layer, w_router_pad[j], _cast_layer_bf16(w_moe_gate, j),
                         _cast_layer_bf16(w_moe_up, j), _cast_layer_bf16(w_moe_down, j), group)
        states_p.append(s_p)
        states_s.append(s_s)
        v_rows.append(v_s.reshape(dec_batch, dec_seq, width))

    y_prompt, y_sample = _final_norm(x, final_gain, t_prompt)
    return (y_prompt.reshape(batch, seq, d), y_sample.reshape(dec_batch, dec_seq, d),
            jnp.stack(states_p), jnp.stack(states_s), jnp.stack(v_rows))
```

```python
import functools

import jax
import jax.numpy as jnp
from jax import lax
from jax.experimental import pallas as pl
from jax.experimental.pallas import tpu as pltpu

F32 = jnp.float32
BF16 = jnp.bfloat16

EPS = 1e-6
EXP_CLIP = 60.0
HGRN_CHUNK = 64
PAST_LEN = 2048
TOP_K = 2

LANES = 128
SUBLANES = 8
VMEM_LIMIT_V7X = 56 * 1024 * 1024

TOKEN_TILE = 768
SLOT_TILE = 512
NORM_TILE = 512
PROLOGUE_ROWS = 256
CAST_BLOCK_BYTES = 8 * 1024 * 1024
FF_TILE = 512
IN_TILE = 1024
OUT_TILE = 1024
ADA_TILE = 1024
DMA_ROWS = 512
DMA_UNROLL = 8


def _params(*semantics):
    return pltpu.CompilerParams(dimension_semantics=semantics, vmem_limit_bytes=VMEM_LIMIT_V7X)


def _silu(x):
    return x * (1.0 / (1.0 + jnp.exp(-x)))


def _rms(x):
    return x * lax.rsqrt(jnp.mean(x * x, axis=-1, keepdims=True) + EPS)


def _per_group(y, m, group, op):
    r, c = y.shape
    return op(y.reshape(r // group, group, c), m[:, None, :]).reshape(r, c)


def _norm_mod(x, sc, sh, group):
    y = _per_group(_rms(x), 1.0 + sc, group, jnp.multiply)
    return _per_group(y, sh, group, jnp.add)


def _row_chunks(n_rows, group):
    size = PROLOGUE_ROWS if n_rows % PROLOGUE_ROWS == 0 else n_rows
    assert size % (group * SUBLANES) == 0 or size == n_rows
    return [(r, size) for r in range(0, n_rows, size)]


def _norm_mod_rows(x_ref, sc_ref, sh_ref, group, r0, nr):
    g0, ng = r0 // group, nr // group
    return _norm_mod(x_ref[pl.ds(r0, nr), :], sc_ref[pl.ds(g0, ng), :], sh_ref[pl.ds(g0, ng), :], group)


def _dot(a, b):
    return jnp.dot(a, b, preferred_element_type=F32)


def _dot_nt(a, b):
    return lax.dot_general(a, b, (((1,), (1,)), ((), ())), preferred_element_type=F32)


def _dot_tn(a, b):
    return lax.dot_general(a, b, (((0,), (0,)), ((), ())), preferred_element_type=F32)


def _split3(x):
    hi = x.astype(BF16)
    r = x - hi.astype(F32)
    mid = r.astype(BF16)
    lo = (r - mid.astype(F32)).astype(BF16)
    return hi, mid, lo


def _dot_f32(a, b):
    a1, a2, a3 = _split3(a)
    b1, b2, b3 = b if isinstance(b, tuple) else _split3(b)
    return (_dot(a1, b1) + (_dot(a1, b2) + _dot(a2, b1))
            + (_dot(a2, b2) + _dot(a1, b3) + _dot(a3, b1)))


def _ada_body(c_ref, w_ref, b_ref, o_ref):
    o_ref[...] = _dot_f32(_silu(c_ref[...]), w_ref[...]) + b_ref[...]


def _ada_modulation(c, w_ada, b_ada):
    depth, d, n = w_ada.shape
    n_streams = c.shape[0]
    s = -(-n_streams // (2 * SUBLANES)) * 2 * SUBLANES
    c = jnp.pad(c, ((0, s - n_streams), (0, 0)))
    tn = min(ADA_TILE, n)
    mod = pl.pallas_call(
        _ada_body,
        grid=(depth, n // tn),
        in_specs=[
            pl.BlockSpec((s, d), lambda l, j: (0, 0)),
            pl.BlockSpec((None, d, tn), lambda l, j: (l, 0, j)),
            pl.BlockSpec((None, 1, tn), lambda l, j: (l, 0, j)),
        ],
        out_specs=pl.BlockSpec((None, s, tn), lambda l, j: (l, 0, j)),
        out_shape=jax.ShapeDtypeStruct((depth, s, n), F32),
        compiler_params=_params("parallel", "parallel"),
        name="ada_modulation",
    )(c, w_ada, b_ada.reshape(depth, 1, n))
    return mod[:, :n_streams]


def _inproj_body(x_ref, sc_ref, sh_ref, w_ref, o_ref, h_scr, *, group):
    @pl.when(pl.program_id(1) == 0)
    def _():
        for r0, nr in _row_chunks(x_ref.shape[0], group):
            h_scr[pl.ds(r0, nr), :] = _norm_mod_rows(x_ref, sc_ref, sh_ref, group, r0, nr).astype(BF16)

    o_ref[...] = _dot(h_scr[...], w_ref[...])


def _in_projection(x, modg, layer, w, group):
    t, d = x.shape
    n = w.shape[1]
    tm, tn = TOKEN_TILE, min(IN_TILE, n)
    gm = tm // group
    return pl.pallas_call(
        functools.partial(_inproj_body, group=group),
        grid=(t // tm, n // tn),
        in_specs=[
            pl.BlockSpec((tm, d), lambda m, j: (m, 0)),
            pl.BlockSpec((None, gm, d), lambda m, j: (layer, m, 1)),
            pl.BlockSpec((None, gm, d), lambda m, j: (layer, m, 0)),
            pl.BlockSpec((d, tn), lambda m, j: (0, j)),
        ],
        out_specs=pl.BlockSpec((tm, tn), lambda m, j: (m, j)),
        out_shape=jax.ShapeDtypeStruct((t, n), F32),
        scratch_shapes=[pltpu.VMEM((tm, d), BF16)],
        compiler_params=_params("parallel", "arbitrary"),
        name="in_projection",
    )(x, modg, modg, w)


def _mixer_body(*refs, tb, chunk, n_heads, pos0, has_state0, emit_v, has_cast):
    (u_ref, v_ref, q_ref, f_ref, i_ref, g_ref, wsp_ref, bsp_ref, vgain_ref, again_ref,
     lb_ref, ogain_ref) = refs[:12]
    refs = refs[12:]
    if has_state0:
        state0_ref, refs = refs[0], refs[1:]
    if has_cast:
        cast_in_ref, refs = refs[0], refs[1:]
    ymix_ref, state_ref = refs[:2]
    refs = refs[2:]
    if emit_v:
        vout_ref, refs = refs[0], refs[1:]
    if has_cast:
        cast_out_ref, refs = refs[0], refs[1:]
    (s_scr,) = refs

    j = pl.program_id(1)
    width = n_heads * LANES

    if has_cast:
        cast_out_ref[...] = cast_in_ref[...].astype(BF16)

    @pl.when(j == 0)
    def _():
        if has_state0:
            s_scr[...] = state0_ref[...]
        else:
            s_scr[...] = jnp.zeros_like(s_scr)

    def head(h):
        return slice(h * LANES, (h + 1) * LANES)

    row = lax.broadcasted_iota(jnp.int32, (tb, tb), 0)
    col = lax.broadcasted_iota(jnp.int32, (tb, tb), 1)
    v = v_ref[...]
    u = u_ref[...]
    vgain = vgain_ref[...]
    gated = []
    sq = jnp.zeros((tb, 1), F32)
    for h in range(n_heads):
        vn = _rms(v[:, head(h)]) * vgain[:, head(h)]
        if emit_v:
            vout_ref[:, head(h)] = vn
        w = jnp.where(row >= col, wsp_ref[h, pl.ds(pos0, tb), pl.ds(pos0, tb)], 0.0)
        mixed = _dot(w.astype(BF16), vn.astype(BF16)) + bsp_ref[:, h:h + 1]
        ga = u[:, head(h)] * mixed
        sq = sq + jnp.sum(ga * ga, axis=-1, keepdims=True)
        gated.append(ga)
    inv = lax.rsqrt(sq / width + EPS)
    again = again_ref[...]
    for h in range(n_heads):
        ymix_ref[:, head(h)] = (gated[h] * inv * again[:, head(h)]).astype(BF16)

    lb = lb_ref[...]
    ogain = ogain_ref[...]
    small = [m for m in (4, 2, 1) if m < SUBLANES]
    n_sel = (1 + len(small)) * chunk
    sel_row = lax.broadcasted_iota(jnp.int32, (n_sel, 3 * chunk), 0)
    sel_col = lax.broadcasted_iota(jnp.int32, (n_sel, 3 * chunk), 1) % chunk
    pos = sel_row % chunk
    bound = pos
    for n, m in enumerate(small):
        bound = jnp.where(sel_row // chunk == n + 1, pos - pos % (2 * m) + m - 1, bound)
    sel = (sel_col <= bound).astype(BF16)
    trow = lax.broadcasted_iota(jnp.int32, (chunk, width), 0)
    upper = {m: trow % (2 * m) >= m for m in small}
    ti = lax.broadcasted_iota(jnp.int32, (chunk, chunk), 0)
    si = lax.broadcasted_iota(jnp.int32, (chunk, chunk), 1)
    same_block = {m: ti - ti % (2 * m) == si - si % (2 * m) for m in small}
    diag = ti == si
    eye = (lax.broadcasted_iota(jnp.int32, (LANES, LANES), 0)
           == lax.broadcasted_iota(jnp.int32, (LANES, LANES), 1))
    for c in range(tb // chunk):
        rows = pl.ds(c * chunk, chunk)
        fx = f_ref[rows, :]
        e = jnp.exp(-jnp.abs(fx))
        log_f = (jnp.minimum(fx, 0.0) - jnp.log1p(e)
                 + jnp.log1p(lb * jnp.exp(jnp.minimum(-fx, EXP_CLIP))))
        kx = (1.0 - lb) * (jnp.where(fx >= 0.0, e, 1.0) / (1.0 + e))
        qs = _silu(q_ref[rows, :])
        vx = i_ref[rows, :]
        sums = _dot(sel, jnp.concatenate(_split3(log_f), axis=0))
        cum = sums[:chunk]

        last = cum[chunk - 1:chunk, :]
        qe = (qs * jnp.exp(cum)).astype(BF16)
        kd = (kx * jnp.exp(last - cum)).astype(BF16)
        el = jnp.exp(last)
        vb = vx.astype(BF16)
        qb = qs.astype(BF16)
        kb = kx.astype(BF16)
        a_small, b_small = [], []
        for n, m in enumerate(small):
            ref = sums[(n + 1) * chunk:(n + 2) * chunk]
            a_small.append(jnp.where(upper[m], qs * jnp.exp(jnp.minimum(cum - ref, 0.0)), 0.0).astype(BF16))
            b_small.append(jnp.where(upper[m], 0.0, kx * jnp.exp(jnp.minimum(ref - cum, 0.0))).astype(BF16))

        scores = [jnp.where(diag, _dot_nt(qb[:, head(h)], kb[:, head(h)]), 0.0) for h in range(n_heads)]
        for n, m in enumerate(small):
            part = [_dot_nt(a_small[n][:, head(h)], b_small[n][:, head(h)]) for h in range(n_heads)]
            scores = [s + jnp.where(same_block[m], p, 0.0) for s, p in zip(scores, part)]
        s_old = [s_scr[h] for h in range(n_heads)]
        read = [_dot(qe[:, head(h)], s_old[h].astype(BF16)) for h in range(n_heads)]
        grow = [_dot_tn(kd[:, head(h)], vb[:, head(h)]) for h in range(n_heads)]
        out = [_dot(scores[h].astype(BF16), vb[:, head(h)]) + read[h] for h in range(n_heads)]
        for h in range(n_heads):
            ecol = jnp.sum(jnp.where(eye, jnp.broadcast_to(el[:, head(h)], (LANES, LANES)), 0.0),
                           axis=1, keepdims=True)
            s_scr[h] = s_old[h] * ecol + grow[h]

        m = chunk // 2
        while m >= SUBLANES:
            blocks = [(b * 2 * m, b * 2 * m + m) for b in range(chunk // (2 * m))]
            sc = []
            for lo, mid in blocks:
                cref = cum[mid - 1:mid, :]
                a_blk = (qs[mid:mid + m, :] * jnp.exp(cum[mid:mid + m, :] - cref)).astype(BF16)
                b_blk = (kx[lo:mid, :] * jnp.exp(cref - cum[lo:mid, :])).astype(BF16)
                sc.append([_dot_nt(a_blk[:, head(h)], b_blk[:, head(h)]) for h in range(n_heads)])
            zero = jnp.zeros((m, LANES), F32)
            for h in range(n_heads):
                pieces = []
                for (lo, mid), s_blk in zip(blocks, sc):
                    pieces += [zero, _dot(s_blk[h].astype(BF16), vb[lo:mid, head(h)])]
                out[h] = out[h] + jnp.concatenate(pieces, axis=0)
            m //= 2

        gx = _silu(g_ref[rows, :])
        for h in range(n_heads):
            yb = _rms(out[h]) * ogain[:, head(h)] * gx[:, head(h)]
            ymix_ref[rows, pl.ds(width + h * LANES, LANES)] = yb.astype(BF16)

    @pl.when(j == pl.num_programs(1) - 1)
    def _():
        state_ref[...] = s_scr[...]


def _cast_rows_per_step(w, n_steps):
    _, n_exp, rows, _ = w.shape
    per_step, rem = divmod(n_exp * rows, n_steps)
    return per_step if rem == 0 and per_step % (2 * SUBLANES) == 0 else None


def _token_mixer(proj, state0, w_sp, b_sp_t, v_gain, a_gain, lb, o_gain, *,
                 n_streams, seq, row0, tb, chunk, n_heads, cast=None):
    width = n_heads * LANES
    nblk = seq // tb
    blk0 = row0 // tb
    has_state0 = state0 is not None
    emit_v = has_state0
    pos0 = PAST_LEN % w_sp.shape[-1] if has_state0 else 0
    assert pos0 % tb == 0

    def rows_map(col):
        return lambda s, j: (blk0 + s * nblk + j, col)

    const2 = lambda s, j: (0, 0)
    in_specs = [pl.BlockSpec((tb, width), rows_map(k)) for k in range(6)]
    in_specs += [
        pl.BlockSpec(w_sp.shape, lambda s, j: (0, 0, 0)),
        pl.BlockSpec((tb, n_heads), lambda s, j: (pos0 // tb, 0)),
        pl.BlockSpec((1, width), const2),
        pl.BlockSpec((1, width), const2),
        pl.BlockSpec((1, width), const2),
        pl.BlockSpec((1, width), const2),
    ]
    args = [proj] * 6 + [w_sp, b_sp_t, v_gain, a_gain, lb, o_gain]
    if has_state0:
        in_specs.append(pl.BlockSpec((None, n_heads, LANES, LANES), lambda s, j: (s, 0, 0, 0)))
        args.append(state0)
    own_rows = lambda s, j: (s * nblk + j, 0)
    out_shape = [jax.ShapeDtypeStruct((n_streams * seq, 2 * width), BF16),
                 jax.ShapeDtypeStruct((n_streams, n_heads, LANES, LANES), F32)]
    out_specs = [pl.BlockSpec((tb, 2 * width), own_rows),
                 pl.BlockSpec((None, n_heads, LANES, LANES), lambda s, j: (s, 0, 0, 0))]
    if emit_v:
        out_shape.append(jax.ShapeDtypeStruct((n_streams * seq, width), F32))
        out_specs.append(pl.BlockSpec((tb, width), own_rows))
    if cast is not None:
        w, w_layer = cast
        n_layers, n_exp, w_rows, w_cols = w.shape
        cr = _cast_rows_per_step(w, n_streams * nblk)
        in_specs.append(pl.BlockSpec((None, cr, w_cols), lambda s, j: (w_layer, s * nblk + j, 0)))
        args.append(w.reshape(n_layers, n_exp * w_rows, w_cols))
        out_shape.append(jax.ShapeDtypeStruct((n_exp * w_rows, w_cols), BF16))
        out_specs.append(pl.BlockSpec((cr, w_cols), own_rows))

    outs = pl.pallas_call(
        functools.partial(_mixer_body, tb=tb, chunk=chunk, n_heads=n_heads, pos0=pos0,
                          has_state0=has_state0, emit_v=emit_v, has_cast=cast is not None),
        grid=(n_streams, nblk),
        in_specs=in_specs,
        out_specs=out_specs,
        out_shape=out_shape,
        scratch_shapes=[
            pltpu.VMEM((n_heads, LANES, LANES), F32),
        ],
        compiler_params=_params("parallel", "arbitrary"),
        name="token_mixer_sample" if has_state0 else "token_mixer_prompt",
    )(*args)
    if cast is not None:
        outs = list(outs[:-1]) + [outs[-1].reshape(cast[0].shape[1:])]
    return outs


def _outproj_body(y_ref, w_ref, x_ref, g_ref, o_ref, *, group):
    y = _dot(y_ref[...], w_ref[...])
    o_ref[...] = x_ref[...] + _per_group(y, g_ref[...], group, jnp.multiply)


def _out_projection(ymix, w, x, modg, layer, group):
    t, k = ymix.shape
    d = w.shape[1]
    tm, tn = TOKEN_TILE, min(OUT_TILE, d)
    ncol = d // tn
    return pl.pallas_call(
        functools.partial(_outproj_body, group=group),
        grid=(t // tm, ncol),
        in_specs=[
            pl.BlockSpec((tm, k), lambda m, j: (m, 0)),
            pl.BlockSpec((k, tn), lambda m, j: (0, j)),
            pl.BlockSpec((tm, tn), lambda m, j: (m, j)),
            pl.BlockSpec((None, tm // group, tn), lambda m, j: (layer, m, 2 * ncol + j)),
        ],
        out_specs=pl.BlockSpec((tm, tn), lambda m, j: (m, j)),
        out_shape=jax.ShapeDtypeStruct((t, d), F32),
        compiler_params=_params("parallel", "arbitrary"),
        name="out_projection",
    )(ymix, w, x, modg)


def _ffn_body(x_ref, sc_ref, sh_ref, wg_ref, wu_ref, wd_ref, gate_ref, o_ref, h_scr, *, group):
    f = pl.program_id(1)

    @pl.when(f == 0)
    def _():
        for r0, nr in _row_chunks(x_ref.shape[0], group):
            h_scr[pl.ds(r0, nr), :] = _norm_mod_rows(x_ref, sc_ref, sh_ref, group, r0, nr).astype(BF16)
        o_ref[...] = jnp.zeros_like(o_ref)

    h = h_scr[...]
    a = _silu(_dot(h, wg_ref[...])) * _dot(h, wu_ref[...])
    o_ref[...] += _dot(a.astype(BF16), wd_ref[...])

    @pl.when(f == pl.num_programs(1) - 1)
    def _():
        for r0, nr in _row_chunks(x_ref.shape[0], group):
            rows, grows = pl.ds(r0, nr), pl.ds(r0 // group, nr // group)
            o_ref[rows, :] = x_ref[rows, :] + _per_group(o_ref[rows, :], gate_ref[grows, :], group,
                                                         jnp.multiply)


def _dense_ffn(x, modg, layer, wg, wu, wd, group):
    t, d = x.shape
    ff = wg.shape[1]
    tm, tf = TOKEN_TILE, FF_TILE
    gm = tm // group
    return pl.pallas_call(
        functools.partial(_ffn_body, group=group),
        grid=(t // tm, ff // tf),
        in_specs=[
            pl.BlockSpec((tm, d), lambda m, f: (m, 0)),
            pl.BlockSpec((None, gm, d), lambda m, f: (layer, m, 4)),
            pl.BlockSpec((None, gm, d), lambda m, f: (layer, m, 3)),
            pl.BlockSpec((d, tf), lambda m, f: (0, f)),
            pl.BlockSpec((d, tf), lambda m, f: (0, f)),
            pl.BlockSpec((tf, d), lambda m, f: (f, 0)),
            pl.BlockSpec((None, gm, d), lambda m, f: (layer, m, 5)),
        ],
        out_specs=pl.BlockSpec((tm, d), lambda m, f: (m, 0)),
        out_shape=jax.ShapeDtypeStruct((t, d), F32),
        scratch_shapes=[pltpu.VMEM((tm, d), BF16)],
        compiler_params=_params("parallel", "arbitrary"),
        name="dense_swiglu",
    )(x, modg, modg, wg, wu, wd, modg)


def _route_body(x_ref, sc_ref, sh_ref, wr_ref, h_ref, route_ref, *, group, n_experts):
    tm, d = x_ref.shape
    cbs = d // LANES
    wr = _split3(wr_ref[...])
    for r0, nr in _row_chunks(tm, group):
        h = _norm_mod_rows(x_ref, sc_ref, sh_ref, group, r0, nr)
        hf = h.astype(BF16).astype(F32)
        for cb in range(cbs):
            h_ref[pl.ds(r0 * cbs + cb, nr, stride=cbs), :] = hf[:, cb * LANES:(cb + 1) * LANES]
        logits = _dot_f32(h, wr)
        lane = lax.broadcasted_iota(jnp.int32, logits.shape, 1)
        neg = jnp.float32(-jnp.inf)
        l1 = jnp.where(lane < n_experts, logits, neg)
        m1 = jnp.max(l1, axis=-1, keepdims=True)
        i1 = jnp.min(jnp.where(l1 == m1, lane, LANES), axis=-1, keepdims=True)
        l2 = jnp.where(lane == i1, neg, l1)
        m2 = jnp.max(l2, axis=-1, keepdims=True)
        i2 = jnp.min(jnp.where(l2 == m2, lane, LANES), axis=-1, keepdims=True)
        p = jnp.exp(m2 - m1)
        w1 = 1.0 / (1.0 + p)
        w2 = p / (1.0 + p)
        route_ref[pl.ds(r0, nr), :] = jnp.where(
            lane == 0, i1.astype(F32),
            jnp.where(lane == 1, i2.astype(F32), jnp.where(lane == 2, w1, jnp.where(lane == 3, w2, 0.0))))


def _route(x, modg, layer, w_router_pad, group, n_experts):
    t, d = x.shape
    tm = TOKEN_TILE
    gm = tm // group
    cbs = d // LANES
    return pl.pallas_call(
        functools.partial(_route_body, group=group, n_experts=n_experts),
        grid=(t // tm,),
        in_specs=[
            pl.BlockSpec((tm, d), lambda m: (m, 0)),
            pl.BlockSpec((None, gm, d), lambda m: (layer, m, 4)),
            pl.BlockSpec((None, gm, d), lambda m: (layer, m, 3)),
            pl.BlockSpec((d, LANES), lambda m: (0, 0)),
        ],
        out_specs=[pl.BlockSpec((tm * cbs, LANES), lambda m: (m, 0)),
                   pl.BlockSpec((tm, LANES), lambda m: (m, 0))],
        out_shape=[jax.ShapeDtypeStruct((t * cbs, LANES), F32),
                   jax.ShapeDtypeStruct((t, LANES), F32)],
        compiler_params=_params("parallel"),
        name="route",
    )(x, modg, modg, w_router_pad)


def _token_copy(src, dst, sem, src_tok, dst_tok, cbs, n=1):
    s0 = pl.multiple_of(src_tok * cbs, cbs)
    d0 = pl.multiple_of(dst_tok * cbs, cbs)
    return pltpu.make_async_copy(src.at[pl.ds(s0, n * cbs)], dst.at[pl.ds(d0, n * cbs)], sem)


def _dispatch_body(dest_ref, h_ref, init_ref, hs_ref, sem, *, rows, cbs):
    del init_ref
    base = pl.program_id(0) * rows

    def issue(r, carry):
        for k in range(TOP_K):
            _token_copy(h_ref, hs_ref, sem, r, dest_ref[TOP_K * (base + r) + k], cbs).start()
        return carry

    lax.fori_loop(0, rows, issue, 0, unroll=DMA_UNROLL)
    for k in range(TOP_K):
        _token_copy(h_ref, hs_ref, sem, 0, 0, cbs, rows).wait()


def _dispatch(h2, dest, n_slots, cbs):
    t = h2.shape[0] // cbs
    rows = DMA_ROWS
    hs0 = jnp.zeros((n_slots * cbs, LANES), h2.dtype)
    return pl.pallas_call(
        functools.partial(_dispatch_body, rows=rows, cbs=cbs),
        grid_spec=pltpu.PrefetchScalarGridSpec(
            num_scalar_prefetch=1,
            grid=(t // rows,),
            in_specs=[pl.BlockSpec((rows * cbs, LANES), lambda m, dest: (m, 0)),
                      pl.BlockSpec(memory_space=pl.ANY)],
            out_specs=pl.BlockSpec(memory_space=pl.ANY),
            scratch_shapes=[pltpu.SemaphoreType.DMA(())],
        ),
        out_shape=jax.ShapeDtypeStruct(hs0.shape, hs0.dtype),
        input_output_aliases={2: 0},
        compiler_params=_params("arbitrary"),
        name="dispatch",
    )(dest, h2, hs0)


def _experts_body(te_ref, nu_ref, hs_ref, wg_ref, wu_ref, wd_ref, o_ref, h_scr, acc_scr):
    del te_ref
    i, f = pl.program_id(0), pl.program_id(1)
    nf = pl.num_programs(1)
    ts, d = h_scr.shape
    cbs = d // LANES
    used = i < nu_ref[0]

    @pl.when(used & (f == 0))
    def _():
        for cb in range(cbs):
            h_scr[:, cb * LANES:(cb + 1) * LANES] = hs_ref[pl.ds(cb, ts, stride=cbs), :].astype(BF16)
        acc_scr[...] = jnp.zeros_like(acc_scr)

    @pl.when(used)
    def _():
        h = h_scr[...]
        a = _silu(_dot(h, wg_ref[...])) * _dot(h, wu_ref[...])
        acc_scr[...] += _dot(a.astype(BF16), wd_ref[...])

    @pl.when(used & (f == nf - 1))
    def _():
        for cb in range(cbs):
            o_ref[pl.ds(cb, ts, stride=cbs), :] = acc_scr[:, cb * LANES:(cb + 1) * LANES]

    @pl.when(jnp.logical_not(used) & (f == nf - 1))
    def _():
        o_ref[...] = jnp.zeros_like(o_ref)


def _experts(hs, tile_expert, n_used, wg, wu, wd):
    n_exp, d, ff = wg.shape
    cbs = d // LANES
    n_slots = hs.shape[0] // cbs
    ts, tf = SLOT_TILE, FF_TILE
    nf = ff // tf

    def fcol(i, f, nu):
        return jnp.where(i < nu[0], f, nf - 1)

    return pl.pallas_call(
        _experts_body,
        grid_spec=pltpu.PrefetchScalarGridSpec(
            num_scalar_prefetch=2,
            grid=(n_slots // ts, nf),
            in_specs=[
                pl.BlockSpec((ts * cbs, LANES), lambda i, f, te, nu: (i, 0)),
                pl.BlockSpec((None, d, tf), lambda i, f, te, nu: (te[i], 0, fcol(i, f, nu))),
                pl.BlockSpec((None, d, tf), lambda i, f, te, nu: (te[i], 0, fcol(i, f, nu))),
                pl.BlockSpec((None, tf, d), lambda i, f, te, nu: (te[i], fcol(i, f, nu), 0)),
            ],
            out_specs=pl.BlockSpec((ts * cbs, LANES), lambda i, f, te, nu: (i, 0)),
            scratch_shapes=[pltpu.VMEM((ts, d), BF16), pltpu.VMEM((ts, d), F32)],
        ),
        out_shape=jax.ShapeDtypeStruct((n_slots * cbs, LANES), F32),
        compiler_params=_params("parallel", "arbitrary"),
        name="experts",
    )(tile_expert, n_used, hs, wg, wu, wd)


def _combine_body(dest_ref, ys_ref, x_ref, route_ref, gate_ref, o_ref, buf, sem, *, group):
    tm, d = x_ref.shape
    cbs = d // LANES
    base = pl.program_id(0) * tm

    def issue(r, carry):
        for k in range(TOP_K):
            _token_copy(ys_ref, buf.at[k], sem, dest_ref[TOP_K * (base + r) + k], r, cbs).start()
        return carry

    lax.fori_loop(0, tm, issue, 0, unroll=DMA_UNROLL)
    for k in range(TOP_K):
        _token_copy(ys_ref, buf.at[k], sem, 0, 0, cbs, tm).wait()
    w = [route_ref[:, TOP_K + k:TOP_K + k + 1] for k in range(TOP_K)]
    gate = gate_ref[...]
    for cb in range(cbs):
        cols = slice(cb * LANES, (cb + 1) * LANES)
        y = w[0] * buf[0, pl.ds(cb, tm, stride=cbs), :] + w[1] * buf[1, pl.ds(cb, tm, stride=cbs), :]
        o_ref[:, cols] = x_ref[:, cols] + _per_group(y, gate[:, cols], group, jnp.multiply)


def _combine(ys, dest, x, route, modg, layer, group):
    t, d = x.shape
    tm = DMA_ROWS
    cbs = d // LANES
    return pl.pallas_call(
        functools.partial(_combine_body, group=group),
        grid_spec=pltpu.PrefetchScalarGridSpec(
            num_scalar_prefetch=1,
            grid=(t // tm,),
            in_specs=[
                pl.BlockSpec(memory_space=pl.ANY),
                pl.BlockSpec((tm, d), lambda m, dest: (m, 0)),
                pl.BlockSpec((tm, LANES), lambda m, dest: (m, 0)),
                pl.BlockSpec((None, tm // group, d), lambda m, dest: (layer, m, 5)),
            ],
            out_specs=pl.BlockSpec((tm, d), lambda m, dest: (m, 0)),
            scratch_shapes=[pltpu.VMEM((TOP_K, tm * cbs, LANES), F32), pltpu.SemaphoreType.DMA(())],
        ),
        out_shape=jax.ShapeDtypeStruct((t, d), F32),
        compiler_params=_params("arbitrary"),
        name="combine",
    )(dest, ys, x, route, modg)


def _slot_plan(route, n_experts, n_tiles):
    ts = SLOT_TILE
    experts = jnp.arange(n_experts, dtype=jnp.int32)
    onehot = [(route[:, k].astype(jnp.int32)[:, None] == experts[None, :]).astype(jnp.int32)
              for k in range(TOP_K)]
    before = []
    offset = jnp.zeros((n_experts,), jnp.int32)
    for oh in onehot:
        cs = jnp.cumsum(oh, axis=0)
        before.append(offset[None, :] + cs - oh)
        offset = offset + cs[-1]
    padded = (offset + ts - 1) // ts * ts
    ends = jnp.cumsum(padded)
    starts = ends - padded
    dest = jnp.stack([jnp.sum(oh * (starts[None, :] + bf), axis=1) for oh, bf in zip(onehot, before)],
                     axis=1).reshape(-1)
    n_used = (ends[-1] // ts).astype(jnp.int32)
    tile_start = jnp.arange(n_tiles, dtype=jnp.int32) * ts
    tile_start = jnp.minimum(tile_start, (n_used - 1) * ts)
    tile_expert = jnp.sum((tile_start[:, None] >= ends[None, :]).astype(jnp.int32), axis=1)
    return dest, tile_expert, n_used.reshape(1)


def _moe_ffn(x, modg, layer, w_router_pad, wg, wu, wd, group):
    t, d = x.shape
    n_exp = wg.shape[0]
    cbs = d // LANES
    ts = SLOT_TILE
    n_tiles = -(-(TOP_K * t + n_exp * (ts - 1)) // ts)
    h2, route = _route(x, modg, layer, w_router_pad, group, n_exp)
    dest, tile_expert, n_used = _slot_plan(route, n_exp, n_tiles)
    hs = _dispatch(h2, dest, n_tiles * ts, cbs)
    ys = _experts(hs, tile_expert, n_used, wg, wu, wd)
    return _combine(ys, dest, x, route, modg, layer, group)


def _final_body(x_ref, gain_ref, op_ref, os_ref, *, n_prompt_tiles):
    y = _rms(x_ref[...]) * gain_ref[...]
    m = pl.program_id(0)

    @pl.when(m < n_prompt_tiles)
    def _():
        op_ref[...] = y

    @pl.when(m >= n_prompt_tiles)
    def _():
        os_ref[...] = y


def _final_norm(x, gain, t_prompt):
    t, d = x.shape
    tm = NORM_TILE
    assert t_prompt % tm == 0 and (t - t_prompt) % tm == 0
    npt = t_prompt // tm
    return pl.pallas_call(
        functools.partial(_final_body, n_prompt_tiles=npt),
        grid=(t // tm,),
        in_specs=[pl.BlockSpec((tm, d), lambda m: (m, 0)), pl.BlockSpec((1, d), lambda m: (0, 0))],
        out_specs=[pl.BlockSpec((tm, d), lambda m: (jnp.minimum(m, npt - 1), 0)),
                   pl.BlockSpec((tm, d), lambda m: (jnp.maximum(m - npt, 0), 0))],
        out_shape=[jax.ShapeDtypeStruct((t_prompt, d), F32),
                   jax.ShapeDtypeStruct((t - t_prompt, d), F32)],
        compiler_params=_params("arbitrary"),
        name="final_norm",
    )(x, gain.reshape(1, d))


def _cast_body(w_ref, o_ref):
    o_ref[...] = w_ref[...].astype(BF16)


def _cast_layer_bf16(w, layer):
    _, n_exp, rows, cols = w.shape
    tr = min(rows, 1 << ((CAST_BLOCK_BYTES // (4 * cols)).bit_length() - 1))
    assert rows % tr == 0 and tr % (2 * SUBLANES) == 0
    return pl.pallas_call(
        _cast_body,
        grid=(n_exp, rows // tr),
        in_specs=[pl.BlockSpec((None, None, tr, cols), lambda e, r: (layer, e, r, 0))],
        out_specs=pl.BlockSpec((None, tr, cols), lambda e, r: (e, r, 0)),
        out_shape=jax.ShapeDtypeStruct((n_exp, rows, cols), BF16),
        compiler_params=_params("parallel", "parallel"),
        name="cast_bf16",
    )(w)


def kernel(x_prompt, x_sample, state_hgrn, c_prompt, c_sample, w_ada, b_ada, w_in, w_out, w_spatial, b_spatial, gmlp_v_gain, gmlp_out_gain, hgrn_lb_logits, hgrn_out_gain, w_ffn_gate, w_ffn_up, w_ffn_down, w_router, w_moe_gate, w_moe_up, w_moe_down, final_gain):
    batch, seq, d = x_prompt.shape
    dec_batch, dec_seq, _ = x_sample.shape
    depth = w_in.shape[0]
    n_heads = w_spatial.shape[1]
    gmlp_chunk = w_spatial.shape[2]
    width = n_heads * LANES
    n_exp = w_router.shape[-1]
    group = dec_seq
    t_prompt = batch * seq
    t = t_prompt + dec_batch * dec_seq
    assert state_hgrn.shape[2:] == (n_heads, LANES, LANES) and gmlp_v_gain.shape[1:] == (n_heads, LANES)
    assert w_in.shape[2] == 6 * width and seq % gmlp_chunk == 0 and gmlp_chunk % HGRN_CHUNK == 0
    assert seq % group == 0 and TOKEN_TILE % group == 0 and t % TOKEN_TILE == 0 and t % DMA_ROWS == 0
    assert dec_seq <= HGRN_CHUNK and PAST_LEN % gmlp_chunk + dec_seq <= gmlp_chunk

    x = jnp.concatenate([x_prompt.reshape(t_prompt, d), x_sample.reshape(-1, d)], axis=0)
    c = jnp.concatenate([c_prompt, c_sample], axis=0)
    mod = _ada_modulation(c, w_ada, b_ada)
    mod_prompt = jnp.broadcast_to(mod[:, :batch, None, :], (depth, batch, seq // group, mod.shape[-1]))
    modg = jnp.concatenate([mod_prompt.reshape(depth, t_prompt // group, -1), mod[:, batch:]], axis=1)

    p = jax.nn.softmax(hgrn_lb_logits.astype(F32), axis=0)
    lbs = jnp.cumsum(p, axis=0) - p[0]
    w_router_pad = jnp.pad(w_router, ((0, 0), (0, 0), (0, LANES - n_exp)))
    b_sp_t = jnp.swapaxes(b_spatial, 1, 2)

    states_p, states_s, v_rows = [], [], []
    for layer in range(depth):
        proj = _in_projection(x, modg, layer, w_in[layer].astype(BF16), group)
        small = (w_spatial[layer], b_sp_t[layer], gmlp_v_gain[layer].reshape(1, width),
                 gmlp_out_gain[layer].reshape(1, width), lbs[layer].reshape(1, width),
                 hgrn_out_gain[layer].reshape(1, width))
        j = layer // 2
        w_ride = w_moe_up if layer % 2 else w_moe_gate
        ride = j < w_ride.shape[0] and _cast_rows_per_step(w_ride, t_prompt // gmlp_chunk) is not None
        y_p, s_p, *cast_out = _token_mixer(proj, None, *small, n_streams=batch, seq=seq, row0=0,
                                           tb=gmlp_chunk, chunk=HGRN_CHUNK, n_heads=n_heads,
                                           cast=(w_ride, j) if ride else None)
        if layer % 2 == 0:
            gate_bf = cast_out[0] if ride else None
        else:
            up_bf = cast_out[0] if ride else None
        y_s, s_s, v_s = _token_mixer(proj, state_hgrn[layer], *small, n_streams=dec_batch,
                                     seq=dec_seq, row0=t_prompt, tb=dec_seq, chunk=dec_seq,
                                     n_heads=n_heads)
        ymix = jnp.concatenate([y_p, y_s], axis=0)
        x = _out_projection(ymix, w_out[layer].astype(BF16), x, modg, layer, group)
        if layer % 2 == 0:
            x = _dense_ffn(x, modg, layer, w_ffn_gate[j].astype(BF16), w_ffn_up[j].astype(BF16),
                           w_ffn_down[j].astype(BF16), group)
        else:
            if gate_bf is None:
                gate_bf = _cast_layer_bf16(w_moe_gate, j)
            if up_bf is None:
                up_bf = _cast_layer_bf16(w_moe_up, j)
            x = _moe_ffn(x, modg, layer, w_router_pad[j], gate_bf, up_bf,
                         _cast_layer_bf16(w_moe_down, j), group)
        states_p.append(s_p)
        states_s.append(s_s)
        v_rows.append(v_s.reshape(dec_batch, dec_seq, width))

    y_prompt, y_sample = _final_norm(x, final_gain, t_prompt)
    return (y_prompt.reshape(batch, seq, d), y_sample.reshape(dec_batch, dec_seq, d),
            jnp.stack(states_p), jnp.stack(states_s), jnp.stack(v_rows))
```

```python
import functools

import jax
import jax.numpy as jnp
from jax import lax
from jax.experimental import pallas as pl
from jax.experimental.pallas import tpu as pltpu

F32 = jnp.float32
BF16 = jnp.bfloat16

EPS = 1e-6
EXP_CLIP = 60.0
HGRN_CHUNK = 64
PAST_LEN = 2048
TOP_K = 2

LANES = 128
SUBLANES = 8
VMEM_LIMIT_V7X = 56 * 1024 * 1024

TOKEN_TILE = 768
SLOT_TILE = 512
NORM_TILE = 512
PROLOGUE_ROWS = 256
CAST_BLOCK_BYTES = 8 * 1024 * 1024
FF_TILE = 512
IN_TILE = 1024
OUT_TILE = 1024
ADA_TILE = 1024
DMA_ROWS = 512
DMA_UNROLL = 8


def _params(*semantics):
    return pltpu.CompilerParams(dimension_semantics=semantics, vmem_limit_bytes=VMEM_LIMIT_V7X)


def _silu(x):
    return x * (1.0 / (1.0 + jnp.exp(-x)))


def _rms(x):
    return x * lax.rsqrt(jnp.mean(x * x, axis=-1, keepdims=True) + EPS)


def _per_group(y, m, group, op):
    r, c = y.shape
    return op(y.reshape(r // group, group, c), m[:, None, :]).reshape(r, c)


def _norm_mod(x, sc, sh, group):
    y = _per_group(_rms(x), 1.0 + sc, group, jnp.multiply)
    return _per_group(y, sh, group, jnp.add)


def _row_chunks(n_rows, group):
    size = PROLOGUE_ROWS if n_rows % PROLOGUE_ROWS == 0 else n_rows
    assert size % (group * SUBLANES) == 0 or size == n_rows
    return [(r, size) for r in range(0, n_rows, size)]


def _norm_mod_rows(x_ref, sc_ref, sh_ref, group, r0, nr):
    g0, ng = r0 // group, nr // group
    return _norm_mod(x_ref[pl.ds(r0, nr), :], sc_ref[pl.ds(g0, ng), :], sh_ref[pl.ds(g0, ng), :], group)


def _dot(a, b):
    return jnp.dot(a, b, preferred_element_type=F32)


def _dot_nt(a, b):
    return lax.dot_general(a, b, (((1,), (1,)), ((), ())), preferred_element_type=F32)


def _dot_tn(a, b):
    return lax.dot_general(a, b, (((0,), (0,)), ((), ())), preferred_element_type=F32)


def _split3(x):
    hi = x.astype(BF16)
    r = x - hi.astype(F32)
    mid = r.astype(BF16)
    lo = (r - mid.astype(F32)).astype(BF16)
    return hi, mid, lo


def _dot_f32(a, b):
    a1, a2, a3 = _split3(a)
    b1, b2, b3 = b if isinstance(b, tuple) else _split3(b)
    return (_dot(a1, b1) + (_dot(a1, b2) + _dot(a2, b1))
            + (_dot(a2, b2) + _dot(a1, b3) + _dot(a3, b1)))


def _ada_body(c_ref, w_ref, b_ref, o_ref):
    o_ref[...] = _dot_f32(_silu(c_ref[...]), w_ref[...]) + b_ref[...]


def _ada_modulation(c, w_ada, b_ada):
    depth, d, n = w_ada.shape
    n_streams = c.shape[0]
    s = -(-n_streams // (2 * SUBLANES)) * 2 * SUBLANES
    c = jnp.pad(c, ((0, s - n_streams), (0, 0)))
    tn = min(ADA_TILE, n)
    mod = pl.pallas_call(
        _ada_body,
        grid=(depth, n // tn),
        in_specs=[
            pl.BlockSpec((s, d), lambda l, j: (0, 0)),
            pl.BlockSpec((None, d, tn), lambda l, j: (l, 0, j)),
            pl.BlockSpec((None, 1, tn), lambda l, j: (l, 0, j)),
        ],
        out_specs=pl.BlockSpec((None, s, tn), lambda l, j: (l, 0, j)),
        out_shape=jax.ShapeDtypeStruct((depth, s, n), F32),
        compiler_params=_params("parallel", "parallel"),
        name="ada_modulation",
    )(c, w_ada, b_ada.reshape(depth, 1, n))
    return mod[:, :n_streams]


def _inproj_body(x_ref, sc_ref, sh_ref, w_ref, o_ref, h_scr, *, group):
    @pl.when(pl.program_id(1) == 0)
    def _():
        for r0, nr in _row_chunks(x_ref.shape[0], group):
            h_scr[pl.ds(r0, nr), :] = _norm_mod_rows(x_ref, sc_ref, sh_ref, group, r0, nr).astype(BF16)

    o_ref[...] = _dot(h_scr[...], w_ref[...])


def _in_projection(x, modg, layer, w, group):
    t, d = x.shape
    n = w.shape[1]
    tm, tn = TOKEN_TILE, min(IN_TILE, n)
    gm = tm // group
    return pl.pallas_call(
        functools.partial(_inproj_body, group=group),
        grid=(t // tm, n // tn),
        in_specs=[
            pl.BlockSpec((tm, d), lambda m, j: (m, 0)),
            pl.BlockSpec((None, gm, d), lambda m, j: (layer, m, 1)),
            pl.BlockSpec((None, gm, d), lambda m, j: (layer, m, 0)),
            pl.BlockSpec((d, tn), lambda m, j: (0, j)),
        ],
        out_specs=pl.BlockSpec((tm, tn), lambda m, j: (m, j)),
        out_shape=jax.ShapeDtypeStruct((t, n), F32),
        scratch_shapes=[pltpu.VMEM((tm, d), BF16)],
        compiler_params=_params("parallel", "arbitrary"),
        name="in_projection",
    )(x, modg, modg, w)


def _mixer_body(*refs, tb, chunk, n_heads, pos0, has_state0, emit_v, has_cast):
    (u_ref, v_ref, q_ref, f_ref, i_ref, g_ref, wsp_ref, bsp_ref, vgain_ref, again_ref,
     lb_ref, ogain_ref) = refs[:12]
    refs = refs[12:]
    if has_state0:
        state0_ref, refs = refs[0], refs[1:]
    if has_cast:
        cast_in_ref, refs = refs[0], refs[1:]
    ymix_ref, state_ref = refs[:2]
    refs = refs[2:]
    if emit_v:
        vout_ref, refs = refs[0], refs[1:]
    if has_cast:
        cast_out_ref, refs = refs[0], refs[1:]
    (s_scr,) = refs

    j = pl.program_id(1)
    width = n_heads * LANES

    if has_cast:
        cast_out_ref[...] = cast_in_ref[...].astype(BF16)

    @pl.when(j == 0)
    def _():
        if has_state0:
            s_scr[...] = state0_ref[...]
        else:
            s_scr[...] = jnp.zeros_like(s_scr)

    def head(h):
        return slice(h * LANES, (h + 1) * LANES)

    row = lax.broadcasted_iota(jnp.int32, (tb, tb), 0)
    col = lax.broadcasted_iota(jnp.int32, (tb, tb), 1)
    v = v_ref[...]
    u = u_ref[...]
    vgain = vgain_ref[...]
    gated = []
    sq = jnp.zeros((tb, 1), F32)
    for h in range(n_heads):
        vn = _rms(v[:, head(h)]) * vgain[:, head(h)]
        if emit_v:
            vout_ref[:, head(h)] = vn
        w = jnp.where(row >= col, wsp_ref[h, pl.ds(pos0, tb), pl.ds(pos0, tb)], 0.0)
        mixed = _dot(w.astype(BF16), vn.astype(BF16)) + bsp_ref[:, h:h + 1]
        ga = u[:, head(h)] * mixed
        sq = sq + jnp.sum(ga * ga, axis=-1, keepdims=True)
        gated.append(ga)
    inv = lax.rsqrt(sq / width + EPS)
    again = again_ref[...]
    for h in range(n_heads):
        ymix_ref[:, head(h)] = (gated[h] * inv * again[:, head(h)]).astype(BF16)

    lb = lb_ref[...]
    ogain = ogain_ref[...]
    small = [m for m in (4, 2, 1) if m < SUBLANES]
    n_sel = (1 + len(small)) * chunk
    sel_row = lax.broadcasted_iota(jnp.int32, (n_sel, 3 * chunk), 0)
    sel_col = lax.broadcasted_iota(jnp.int32, (n_sel, 3 * chunk), 1) % chunk
    pos = sel_row % chunk
    bound = pos
    for n, m in enumerate(small):
        bound = jnp.where(sel_row // chunk == n + 1, pos - pos % (2 * m) + m - 1, bound)
    sel = (sel_col <= bound).astype(BF16)
    trow = lax.broadcasted_iota(jnp.int32, (chunk, width), 0)
    upper = {m: trow % (2 * m) >= m for m in small}
    ti = lax.broadcasted_iota(jnp.int32, (chunk, chunk), 0)
    si = lax.broadcasted_iota(jnp.int32, (chunk, chunk), 1)
    same_block = {m: ti - ti % (2 * m) == si - si % (2 * m) for m in small}
    diag = ti == si
    eye = (lax.broadcasted_iota(jnp.int32, (LANES, LANES), 0)
           == lax.broadcasted_iota(jnp.int32, (LANES, LANES), 1))
    for c in range(tb // chunk):
        rows = pl.ds(c * chunk, chunk)
        fx = f_ref[rows, :]
        e = jnp.exp(-jnp.abs(fx))
        log_f = (jnp.minimum(fx, 0.0) - jnp.log1p(e)
                 + jnp.log1p(lb * jnp.exp(jnp.minimum(-fx, EXP_CLIP))))
        kx = (1.0 - lb) * (jnp.where(fx >= 0.0, e, 1.0) / (1.0 + e))
        qs = _silu(q_ref[rows, :])
        vx = i_ref[rows, :]
        sums = _dot(sel, jnp.concatenate(_split3(log_f), axis=0))
        cum = sums[:chunk]

        last = cum[chunk - 1:chunk, :]
        qe = (qs * jnp.exp(cum)).astype(BF16)
        kd = (kx * jnp.exp(last - cum)).astype(BF16)
        el = jnp.exp(last)
        vb = vx.astype(BF16)
        qb = qs.astype(BF16)
        kb = kx.astype(BF16)
        a_small, b_small = [], []
        for n, m in enumerate(small):
            ref = sums[(n + 1) * chunk:(n + 2) * chunk]
            a_small.append(jnp.where(upper[m], qs * jnp.exp(jnp.minimum(cum - ref, 0.0)), 0.0).astype(BF16))
            b_small.append(jnp.where(upper[m], 0.0, kx * jnp.exp(jnp.minimum(ref - cum, 0.0))).astype(BF16))

        scores = [jnp.where(diag, _dot_nt(qb[:, head(h)], kb[:, head(h)]), 0.0) for h in range(n_heads)]
        for n, m in enumerate(small):
            part = [_dot_nt(a_small[n][:, head(h)], b_small[n][:, head(h)]) for h in range(n_heads)]
            scores = [s + jnp.where(same_block[m], p, 0.0) for s, p in zip(scores, part)]
        s_old = [s_scr[h] for h in range(n_heads)]
        read = [_dot(qe[:, head(h)], s_old[h].astype(BF16)) for h in range(n_heads)]
        grow = [_dot_tn(kd[:, head(h)], vb[:, head(h)]) for h in range(n_heads)]
        out = [_dot(scores[h].astype(BF16), vb[:, head(h)]) + read[h] for h in range(n_heads)]
        for h in range(n_heads):
            ecol = jnp.sum(jnp.where(eye, jnp.broadcast_to(el[:, head(h)], (LANES, LANES)), 0.0),
                           axis=1, keepdims=True)
            s_scr[h] = s_old[h] * ecol + grow[h]

        m = chunk // 2
        while m >= SUBLANES:
            blocks = [(b * 2 * m, b * 2 * m + m) for b in range(chunk // (2 * m))]
            sc = []
            for lo, mid in blocks:
                cref = cum[mid - 1:mid, :]
                a_blk = (qs[mid:mid + m, :] * jnp.exp(cum[mid:mid + m, :] - cref)).astype(BF16)
                b_blk = (kx[lo:mid, :] * jnp.exp(cref - cum[lo:mid, :])).astype(BF16)
                sc.append([_dot_nt(a_blk[:, head(h)], b_blk[:, head(h)]) for h in range(n_heads)])
            zero = jnp.zeros((m, LANES), F32)
            for h in range(n_heads):
                pieces = []
                for (lo, mid), s_blk in zip(blocks, sc):
                    pieces += [zero, _dot(s_blk[h].astype(BF16), vb[lo:mid, head(h)])]
                out[h] = out[h] + jnp.concatenate(pieces, axis=0)
            m //= 2

        gx = _silu(g_ref[rows, :])
        for h in range(n_heads):
            yb = _rms(out[h]) * ogain[:, head(h)] * gx[:, head(h)]
            ymix_ref[rows, pl.ds(width + h * LANES, LANES)] = yb.astype(BF16)

    @pl.when(j == pl.num_programs(1) - 1)
    def _():
        state_ref[...] = s_scr[...]


def _cast_rows_per_step(w, n_steps):
    _, n_exp, rows, _ = w.shape
    per_step, rem = divmod(n_exp * rows, n_steps)
    return per_step if rem == 0 and per_step % (2 * SUBLANES) == 0 else None


def _token_mixer(proj, state0, w_sp, b_sp_t, v_gain, a_gain, lb, o_gain, *,
                 n_streams, seq, row0, tb, chunk, n_heads, cast=None):
    width = n_heads * LANES
    nblk = seq // tb
    blk0 = row0 // tb
    has_state0 = state0 is not None
    emit_v = has_state0
    pos0 = PAST_LEN % w_sp.shape[-1] if has_state0 else 0
    assert pos0 % tb == 0

    def rows_map(col):
        return lambda s, j: (blk0 + s * nblk + j, col)

    const2 = lambda s, j: (0, 0)
    in_specs = [pl.BlockSpec((tb, width), rows_map(k)) for k in range(6)]
    in_specs += [
        pl.BlockSpec(w_sp.shape, lambda s, j: (0, 0, 0)),
        pl.BlockSpec((tb, n_heads), lambda s, j: (pos0 // tb, 0)),
        pl.BlockSpec((1, width), const2),
        pl.BlockSpec((1, width), const2),
        pl.BlockSpec((1, width), const2),
        pl.BlockSpec((1, width), const2),
    ]
    args = [proj] * 6 + [w_sp, b_sp_t, v_gain, a_gain, lb, o_gain]
    if has_state0:
        in_specs.append(pl.BlockSpec((None, n_heads, LANES, LANES), lambda s, j: (s, 0, 0, 0)))
        args.append(state0)
    own_rows = lambda s, j: (s * nblk + j, 0)
    out_shape = [jax.ShapeDtypeStruct((n_streams * seq, 2 * width), BF16),
                 jax.ShapeDtypeStruct((n_streams, n_heads, LANES, LANES), F32)]
    out_specs = [pl.BlockSpec((tb, 2 * width), own_rows),
                 pl.BlockSpec((None, n_heads, LANES, LANES), lambda s, j: (s, 0, 0, 0))]
    if emit_v:
        out_shape.append(jax.ShapeDtypeStruct((n_streams * seq, width), F32))
        out_specs.append(pl.BlockSpec((tb, width), own_rows))
    if cast is not None:
        w, w_layer = cast
        n_layers, n_exp, w_rows, w_cols = w.shape
        cr = _cast_rows_per_step(w, n_streams * nblk)
        in_specs.append(pl.BlockSpec((None, cr, w_cols), lambda s, j: (w_layer, s * nblk + j, 0)))
        args.append(w.reshape(n_layers, n_exp * w_rows, w_cols))
        out_shape.append(jax.ShapeDtypeStruct((n_exp * w_rows, w_cols), BF16))
        out_specs.append(pl.BlockSpec((cr, w_cols), own_rows))

    outs = pl.pallas_call(
        functools.partial(_mixer_body, tb=tb, chunk=chunk, n_heads=n_heads, pos0=pos0,
                          has_state0=has_state0, emit_v=emit_v, has_cast=cast is not None),
        grid=(n_streams, nblk),
        in_specs=in_specs,
        out_specs=out_specs,
        out_shape=out_shape,
        scratch_shapes=[
            pltpu.VMEM((n_heads, LANES, LANES), F32),
        ],
        compiler_params=_params("parallel", "arbitrary"),
        name="token_mixer_sample" if has_state0 else "token_mixer_prompt",
    )(*args)
    if cast is not None:
        outs = list(outs[:-1]) + [outs[-1].reshape(cast[0].shape[1:])]
    return outs


def _outproj_body(y_ref, w_ref, x_ref, g_ref, o_ref, *, group):
    y = _dot(y_ref[...], w_ref[...])
    o_ref[...] = x_ref[...] + _per_group(y, g_ref[...], group, jnp.multiply)


def _out_projection(ymix, w, x, modg, layer, group):
    t, k = ymix.shape
    d = w.shape[1]
    tm, tn = TOKEN_TILE, min(OUT_TILE, d)
    ncol = d // tn
    return pl.pallas_call(
        functools.partial(_outproj_body, group=group),
        grid=(t // tm, ncol),
        in_specs=[
            pl.BlockSpec((tm, k), lambda m, j: (m, 0)),
            pl.BlockSpec((k, tn), lambda m, j: (0, j)),
            pl.BlockSpec((tm, tn), lambda m, j: (m, j)),
            pl.BlockSpec((None, tm // group, tn), lambda m, j: (layer, m, 2 * ncol + j)),
        ],
        out_specs=pl.BlockSpec((tm, tn), lambda m, j: (m, j)),
        out_shape=jax.ShapeDtypeStruct((t, d), F32),
        compiler_params=_params("parallel", "arbitrary"),
        name="out_projection",
    )(ymix, w, x, modg)


def _ffn_body(x_ref, sc_ref, sh_ref, wg_ref, wu_ref, wd_ref, gate_ref, o_ref, h_scr, *, group):
    f = pl.program_id(1)

    @pl.when(f == 0)
    def _():
        for r0, nr in _row_chunks(x_ref.shape[0], group):
            h_scr[pl.ds(r0, nr), :] = _norm_mod_rows(x_ref, sc_ref, sh_ref, group, r0, nr).astype(BF16)
        o_ref[...] = jnp.zeros_like(o_ref)

    h = h_scr[...]
    a = _silu(_dot(h, wg_ref[...])) * _dot(h, wu_ref[...])
    o_ref[...] += _dot(a.astype(BF16), wd_ref[...])

    @pl.when(f == pl.num_programs(1) - 1)
    def _():
        for r0, nr in _row_chunks(x_ref.shape[0], group):
            rows, grows = pl.ds(r0, nr), pl.ds(r0 // group, nr // group)
            o_ref[rows, :] = x_ref[rows, :] + _per_group(o_ref[rows, :], gate_ref[grows, :], group,
                                                         jnp.multiply)


def _dense_ffn(x, modg, layer, wg, wu, wd, group):
    t, d = x.shape
    ff = wg.shape[1]
    tm, tf = TOKEN_TILE, FF_TILE
    gm = tm // group
    return pl.pallas_call(
        functools.partial(_ffn_body, group=group),
        grid=(t // tm, ff // tf),
        in_specs=[
            pl.BlockSpec((tm, d), lambda m, f: (m, 0)),
            pl.BlockSpec((None, gm, d), lambda m, f: (layer, m, 4)),
            pl.BlockSpec((None, gm, d), lambda m, f: (layer, m, 3)),
            pl.BlockSpec((d, tf), lambda m, f: (0, f)),
            pl.BlockSpec((d, tf), lambda m, f: (0, f)),
            pl.BlockSpec((tf, d), lambda m, f: (f, 0)),
            pl.BlockSpec((None, gm, d), lambda m, f: (layer, m, 5)),
        ],
        out_specs=pl.BlockSpec((tm, d), lambda m, f: (m, 0)),
        out_shape=jax.ShapeDtypeStruct((t, d), F32),
        scratch_shapes=[pltpu.VMEM((tm, d), BF16)],
        compiler_params=_params("parallel", "arbitrary"),
        name="dense_swiglu",
    )(x, modg, modg, wg, wu, wd, modg)


def _route_body(x_ref, sc_ref, sh_ref, wr_ref, h_ref, route_ref, *, group, n_experts):
    tm, d = x_ref.shape
    cbs = d // LANES
    wr = _split3(wr_ref[...])
    for r0, nr in _row_chunks(tm, group):
        h = _norm_mod_rows(x_ref, sc_ref, sh_ref, group, r0, nr)
        hf = h.astype(BF16).astype(F32)
        for cb in range(cbs):
            h_ref[pl.ds(r0 * cbs + cb, nr, stride=cbs), :] = hf[:, cb * LANES:(cb + 1) * LANES]
        logits = _dot_f32(h, wr)
        lane = lax.broadcasted_iota(jnp.int32, logits.shape, 1)
        neg = jnp.float32(-jnp.inf)
        l1 = jnp.where(lane < n_experts, logits, neg)
        m1 = jnp.max(l1, axis=-1, keepdims=True)
        i1 = jnp.min(jnp.where(l1 == m1, lane, LANES), axis=-1, keepdims=True)
        l2 = jnp.where(lane == i1, neg, l1)
        m2 = jnp.max(l2, axis=-1, keepdims=True)
        i2 = jnp.min(jnp.where(l2 == m2, lane, LANES), axis=-1, keepdims=True)
        p = jnp.exp(m2 - m1)
        w1 = 1.0 / (1.0 + p)
        w2 = p / (1.0 + p)
        route_ref[pl.ds(r0, nr), :] = jnp.where(
            lane == 0, i1.astype(F32),
            jnp.where(lane == 1, i2.astype(F32), jnp.where(lane == 2, w1, jnp.where(lane == 3, w2, 0.0))))


def _route(x, modg, layer, w_router_pad, group, n_experts):
    t, d = x.shape
    tm = TOKEN_TILE
    gm = tm // group
    cbs = d // LANES
    return pl.pallas_call(
        functools.partial(_route_body, group=group, n_experts=n_experts),
        grid=(t // tm,),
        in_specs=[
            pl.BlockSpec((tm, d), lambda m: (m, 0)),
            pl.BlockSpec((None, gm, d), lambda m: (layer, m, 4)),
            pl.BlockSpec((None, gm, d), lambda m: (layer, m, 3)),
            pl.BlockSpec((d, LANES), lambda m: (0, 0)),
        ],
        out_specs=[pl.BlockSpec((tm * cbs, LANES), lambda m: (m, 0)),
                   pl.BlockSpec((tm, LANES), lambda m: (m, 0))],
        out_shape=[jax.ShapeDtypeStruct((t * cbs, LANES), F32),
                   jax.ShapeDtypeStruct((t, LANES), F32)],
        compiler_params=_params("parallel"),
        name="route",
    )(x, modg, modg, w_router_pad)


def _token_copy(src, dst, sem, src_tok, dst_tok, cbs, n=1):
    s0 = pl.multiple_of(src_tok * cbs, cbs)
    d0 = pl.multiple_of(dst_tok * cbs, cbs)
    return pltpu.make_async_copy(src.at[pl.ds(s0, n * cbs)], dst.at[pl.ds(d0, n * cbs)], sem)


def _dispatch_body(dest_ref, h_ref, init_ref, hs_ref, sem, *, rows, cbs):
    del init_ref
    base = pl.program_id(0) * rows

    def issue(r, carry):
        for k in range(TOP_K):
            _token_copy(h_ref, hs_ref, sem, r, dest_ref[TOP_K * (base + r) + k], cbs).start(priority=k % 2)
        return carry

    lax.fori_loop(0, rows, issue, 0, unroll=DMA_UNROLL)
    for k in range(TOP_K):
        _token_copy(h_ref, hs_ref, sem, 0, 0, cbs, rows).wait()


def _dispatch(h2, dest, n_slots, cbs):
    t = h2.shape[0] // cbs
    rows = DMA_ROWS
    hs0 = jnp.zeros((n_slots * cbs, LANES), h2.dtype)
    return pl.pallas_call(
        functools.partial(_dispatch_body, rows=rows, cbs=cbs),
        grid_spec=pltpu.PrefetchScalarGridSpec(
            num_scalar_prefetch=1,
            grid=(t // rows,),
            in_specs=[pl.BlockSpec((rows * cbs, LANES), lambda m, dest: (m, 0)),
                      pl.BlockSpec(memory_space=pl.ANY)],
            out_specs=pl.BlockSpec(memory_space=pl.ANY),
            scratch_shapes=[pltpu.SemaphoreType.DMA(())],
        ),
        out_shape=jax.ShapeDtypeStruct(hs0.shape, hs0.dtype),
        input_output_aliases={2: 0},
        compiler_params=_params("arbitrary"),
        name="dispatch",
    )(dest, h2, hs0)


def _experts_body(te_ref, nu_ref, hs_ref, wg_ref, wu_ref, wd_ref, o_ref, h_scr, acc_scr):
    del te_ref
    i, f = pl.program_id(0), pl.program_id(1)
    nf = pl.num_programs(1)
    ts, d = h_scr.shape
    cbs = d // LANES
    used = i < nu_ref[0]

    @pl.when(used & (f == 0))
    def _():
        for cb in range(cbs):
            h_scr[:, cb * LANES:(cb + 1) * LANES] = hs_ref[pl.ds(cb, ts, stride=cbs), :].astype(BF16)
        acc_scr[...] = jnp.zeros_like(acc_scr)

    @pl.when(used)
    def _():
        h = h_scr[...]
        a = _silu(_dot(h, wg_ref[...])) * _dot(h, wu_ref[...])
        acc_scr[...] += _dot(a.astype(BF16), wd_ref[...])

    @pl.when(used & (f == nf - 1))
    def _():
        for cb in range(cbs):
            o_ref[pl.ds(cb, ts, stride=cbs), :] = acc_scr[:, cb * LANES:(cb + 1) * LANES]

    @pl.when(jnp.logical_not(used) & (f == nf - 1))
    def _():
        o_ref[...] = jnp.zeros_like(o_ref)


def _experts(hs, tile_expert, n_used, wg, wu, wd):
    n_exp, d, ff = wg.shape
    cbs = d // LANES
    n_slots = hs.shape[0] // cbs
    ts, tf = SLOT_TILE, FF_TILE
    nf = ff // tf

    def fcol(i, f, nu):
        return jnp.where(i < nu[0], f, nf - 1)

    return pl.pallas_call(
        _experts_body,
        grid_spec=pltpu.PrefetchScalarGridSpec(
            num_scalar_prefetch=2,
            grid=(n_slots // ts, nf),
            in_specs=[
                pl.BlockSpec((ts * cbs, LANES), lambda i, f, te, nu: (i, 0)),
                pl.BlockSpec((None, d, tf), lambda i, f, te, nu: (te[i], 0, fcol(i, f, nu))),
                pl.BlockSpec((None, d, tf), lambda i, f, te, nu: (te[i], 0, fcol(i, f, nu))),
                pl.BlockSpec((None, tf, d), lambda i, f, te, nu: (te[i], fcol(i, f, nu), 0)),
            ],
            out_specs=pl.BlockSpec((ts * cbs, LANES), lambda i, f, te, nu: (i, 0)),
            scratch_shapes=[pltpu.VMEM((ts, d), BF16), pltpu.VMEM((ts, d), F32)],
        ),
        out_shape=jax.ShapeDtypeStruct((n_slots * cbs, LANES), F32),
        compiler_params=_params("parallel", "arbitrary"),
        name="experts",
    )(tile_expert, n_used, hs, wg, wu, wd)


def _combine_body(dest_ref, ys_ref, x_ref, route_ref, gate_ref, o_ref, buf, sem, *, group):
    tm, d = x_ref.shape
    cbs = d // LANES
    base = pl.program_id(0) * tm

    def issue(r, carry):
        for k in range(TOP_K):
            _token_copy(ys_ref, buf.at[k], sem, dest_ref[TOP_K * (base + r) + k], r, cbs).start(priority=k % 2)
        return carry

    lax.fori_loop(0, tm, issue, 0, unroll=DMA_UNROLL)
    for k in range(TOP_K):
        _token_copy(ys_ref, buf.at[k], sem, 0, 0, cbs, tm).wait()
    w = [route_ref[:, TOP_K + k:TOP_K + k + 1] for k in range(TOP_K)]
    gate = gate_ref[...]
    for cb in range(cbs):
        cols = slice(cb * LANES, (cb + 1) * LANES)
        y = w[0] * buf[0, pl.ds(cb, tm, stride=cbs), :] + w[1] * buf[1, pl.ds(cb, tm, stride=cbs), :]
        o_ref[:, cols] = x_ref[:, cols] + _per_group(y, gate[:, cols], group, jnp.multiply)


def _combine(ys, dest, x, route, modg, layer, group):
    t, d = x.shape
    tm = DMA_ROWS
    cbs = d // LANES
    return pl.pallas_call(
        functools.partial(_combine_body, group=group),
        grid_spec=pltpu.PrefetchScalarGridSpec(
            num_scalar_prefetch=1,
            grid=(t // tm,),
            in_specs=[
                pl.BlockSpec(memory_space=pl.ANY),
                pl.BlockSpec((tm, d), lambda m, dest: (m, 0)),
                pl.BlockSpec((tm, LANES), lambda m, dest: (m, 0)),
                pl.BlockSpec((None, tm // group, d), lambda m, dest: (layer, m, 5)),
            ],
            out_specs=pl.BlockSpec((tm, d), lambda m, dest: (m, 0)),
            scratch_shapes=[pltpu.VMEM((TOP_K, tm * cbs, LANES), F32), pltpu.SemaphoreType.DMA(())],
        ),
        out_shape=jax.ShapeDtypeStruct((t, d), F32),
        compiler_params=_params("arbitrary"),
        name="combine",
    )(dest, ys, x, route, modg)


def _slot_plan(route, n_experts, n_tiles):
    ts = SLOT_TILE
    experts = jnp.arange(n_experts, dtype=jnp.int32)
    onehot = [(route[:, k].astype(jnp.int32)[:, None] == experts[None, :]).astype(jnp.int32)
              for k in range(TOP_K)]
    before = []
    offset = jnp.zeros((n_experts,), jnp.int32)
    for oh in onehot:
        cs = jnp.cumsum(oh, axis=0)
        before.append(offset[None, :] + cs - oh)
        offset = offset + cs[-1]
    padded = (offset + ts - 1) // ts * ts
    ends = jnp.cumsum(padded)
    starts = ends - padded
    dest = jnp.stack([jnp.sum(oh * (starts[None, :] + bf), axis=1) for oh, bf in zip(onehot, before)],
                     axis=1).reshape(-1)
    n_used = (ends[-1] // ts).astype(jnp.int32)
    tile_start = jnp.arange(n_tiles, dtype=jnp.int32) * ts
    tile_start = jnp.minimum(tile_start, (n_used - 1) * ts)
    tile_expert = jnp.sum((tile_start[:, None] >= ends[None, :]).astype(jnp.int32), axis=1)
    return dest, tile_expert, n_used.reshape(1)


def _moe_ffn(x, modg, layer, w_router_pad, wg, wu, wd, group):
    t, d = x.shape
    n_exp = wg.shape[0]
    cbs = d // LANES
    ts = SLOT_TILE
    n_tiles = -(-(TOP_K * t + n_exp * (ts - 1)) // ts)
    h2, route = _route(x, modg, layer, w_router_pad, group, n_exp)
    dest, tile_expert, n_used = _slot_plan(route, n_exp, n_tiles)
    hs = _dispatch(h2, dest, n_tiles * ts, cbs)
    ys = _experts(hs, tile_expert, n_used, wg, wu, wd)
    return _combine(ys, dest, x, route, modg, layer, group)


def _final_body(x_ref, gain_ref, op_ref, os_ref, *, n_prompt_tiles):
    y = _rms(x_ref[...]) * gain_ref[...]
    m = pl.program_id(0)

    @pl.when(m < n_prompt_tiles)
    def _():
        op_ref[...] = y

    @pl.when(m >= n_prompt_tiles)
    def _():
        os_ref[...] = y


def _final_norm(x, gain, t_prompt):
    t, d = x.shape
    tm = NORM_TILE
    assert t_prompt % tm == 0 and (t - t_prompt) % tm == 0
    npt = t_prompt // tm
    return pl.pallas_call(
        functools.partial(_final_body, n_prompt_tiles=npt),
        grid=(t // tm,),
        in_specs=[pl.BlockSpec((tm, d), lambda m: (m, 0)), pl.BlockSpec((1, d), lambda m: (0, 0))],
        out_specs=[pl.BlockSpec((tm, d), lambda m: (jnp.minimum(m, npt - 1), 0)),
                   pl.BlockSpec((tm, d), lambda m: (jnp.maximum(m - npt, 0), 0))],
        out_shape=[jax.ShapeDtypeStruct((t_prompt, d), F32),
                   jax.ShapeDtypeStruct((t - t_prompt, d), F32)],
        compiler_params=_params("arbitrary"),
        name="final_norm",
    )(x, gain.reshape(1, d))


def _cast_body(w_ref, o_ref):
    o_ref[...] = w_ref[...].astype(BF16)


def _cast_layer_bf16(w, layer):
    _, n_exp, rows, cols = w.shape
    tr = min(rows, 1 << ((CAST_BLOCK_BYTES // (4 * cols)).bit_length() - 1))
    assert rows % tr == 0 and tr % (2 * SUBLANES) == 0
    return pl.pallas_call(
        _cast_body,
        grid=(n_exp, rows // tr),
        in_specs=[pl.BlockSpec((None, None, tr, cols), lambda e, r: (layer, e, r, 0))],
        out_specs=pl.BlockSpec((None, tr, cols), lambda e, r: (e, r, 0)),
        out_shape=jax.ShapeDtypeStruct((n_exp, rows, cols), BF16),
        compiler_params=_params("parallel", "parallel"),
        name="cast_bf16",
    )(w)


def kernel(x_prompt, x_sample, state_hgrn, c_prompt, c_sample, w_ada, b_ada, w_in, w_out, w_spatial, b_spatial, gmlp_v_gain, gmlp_out_gain, hgrn_lb_logits, hgrn_out_gain, w_ffn_gate, w_ffn_up, w_ffn_down, w_router, w_moe_gate, w_moe_up, w_moe_down, final_gain):
    batch, seq, d = x_prompt.shape
    dec_batch, dec_seq, _ = x_sample.shape
    depth = w_in.shape[0]
    n_heads = w_spatial.shape[1]
    gmlp_chunk = w_spatial.shape[2]
    width = n_heads * LANES
    n_exp = w_router.shape[-1]
    group = dec_seq
    t_prompt = batch * seq
    t = t_prompt + dec_batch * dec_seq
    assert state_hgrn.shape[2:] == (n_heads, LANES, LANES) and gmlp_v_gain.shape[1:] == (n_heads, LANES)
    assert w_in.shape[2] == 6 * width and seq % gmlp_chunk == 0 and gmlp_chunk % HGRN_CHUNK == 0
    assert seq % group == 0 and TOKEN_TILE % group == 0 and t % TOKEN_TILE == 0 and t % DMA_ROWS == 0
    assert dec_seq <= HGRN_CHUNK and PAST_LEN % gmlp_chunk + dec_seq <= gmlp_chunk

    x = jnp.concatenate([x_prompt.reshape(t_prompt, d), x_sample.reshape(-1, d)], axis=0)
    c = jnp.concatenate([c_prompt, c_sample], axis=0)
    mod = _ada_modulation(c, w_ada, b_ada)
    mod_prompt = jnp.broadcast_to(mod[:, :batch, None, :], (depth, batch, seq // group, mod.shape[-1]))
    modg = jnp.concatenate([mod_prompt.reshape(depth, t_prompt // group, -1), mod[:, batch:]], axis=1)

    p = jax.nn.softmax(hgrn_lb_logits.astype(F32), axis=0)
    lbs = jnp.cumsum(p, axis=0) - p[0]
    w_router_pad = jnp.pad(w_router, ((0, 0), (0, 0), (0, LANES - n_exp)))
    b_sp_t = jnp.swapaxes(b_spatial, 1, 2)

    states_p, states_s, v_rows = [], [], []
    for layer in range(depth):
        proj = _in_projection(x, modg, layer, w_in[layer].astype(BF16), group)
        small = (w_spatial[layer], b_sp_t[layer], gmlp_v_gain[layer].reshape(1, width),
                 gmlp_out_gain[layer].reshape(1, width), lbs[layer].reshape(1, width),
                 hgrn_out_gain[layer].reshape(1, width))
        j = layer // 2
        w_ride = w_moe_up if layer % 2 else w_moe_gate
        ride = j < w_ride.shape[0] and _cast_rows_per_step(w_ride, t_prompt // gmlp_chunk) is not None
        y_p, s_p, *cast_out = _token_mixer(proj, None, *small, n_streams=batch, seq=seq, row0=0,
                                           tb=gmlp_chunk, chunk=HGRN_CHUNK, n_heads=n_heads,
                                           cast=(w_ride, j) if ride else None)
        if layer % 2 == 0:
            gate_bf = cast_out[0] if ride else None
        else:
            up_bf = cast_out[0] if ride else None
        y_s, s_s, v_s = _token_mixer(proj, state_hgrn[layer], *small, n_streams=dec_batch,
                                     seq=dec_seq, row0=t_prompt, tb=dec_seq, chunk=dec_seq,
                                     n_heads=n_heads)
        ymix = jnp.concatenate([y_p, y_s], axis=0)
        x = _out_projection(ymix, w_out[layer].astype(BF16), x, modg, layer, group)
        if layer % 2 == 0:
            x = _dense_ffn(x, modg, layer, w_ffn_gate[j].astype(BF16), w_ffn_up[j].astype(BF16),
                           w_ffn_down[j].astype(BF16), group)
        else:
            if gate_bf is None:
                gate_bf = _cast_layer_bf16(w_moe_gate, j)
            if up_bf is None:
                up_bf = _cast_layer_bf16(w_moe_up, j)
            x = _moe_ffn(x, modg, layer, w_router_pad[j], gate_bf, up_bf,
                         _cast_layer_bf16(w_moe_down, j), group)
        states_p.append(s_p)
        states_s.append(s_s)
        v_rows.append(v_s.reshape(dec_batch, dec_seq, width))

    y_prompt, y_sample = _final_norm(x, final_gain, t_prompt)
    return (y_prompt.reshape(batch, seq, d), y_sample.reshape(dec_batch, dec_seq, d),
            jnp.stack(states_p), jnp.stack(states_s), jnp.stack(v_rows))
```
